```python
import jax, jax.numpy as jnp
from jax import lax
import numpy as np

D_MODEL = 2048
BATCH = 8
SEQ = 4096
DEPTH = 4

N_MIXERS = 2
N_RET_LAYERS = (DEPTH + 1) // 2
N_SWA_LAYERS = DEPTH // 2

RET_HEADS = 8
RET_QK_DIM = D_MODEL // RET_HEADS
RET_V_DIM = 2 * RET_QK_DIM
RET_VALUE_WIDTH = RET_HEADS * RET_V_DIM
RET_CHUNK = 128
RET_IN = 2 * RET_HEADS * RET_QK_DIM + 2 * RET_VALUE_WIDTH

SWA_HEAD_DIM = 64
SWA_HEADS = D_MODEL // SWA_HEAD_DIM
SWA_KV_HEADS = 4
SWA_GROUP = SWA_HEADS // SWA_KV_HEADS
SWA_WINDOW = 128
SWA_Q_WIDTH = SWA_HEADS * SWA_HEAD_DIM
SWA_KV_WIDTH = SWA_KV_HEADS * SWA_HEAD_DIM
SWA_IN = SWA_Q_WIDTH + 2 * SWA_KV_WIDTH

FFN_HIDDEN = ((-(-8 * D_MODEL // 3)) + 255) // 256 * 256

PLE_DIM = 256

NORM_EPS = 1e-6

kernel_name = "hybrid_retention_swa_sink_trunk"


def rmsnorm(x, gain):
    xf = x.astype(jnp.float32)
    inv = lax.rsqrt(jnp.mean(xf * xf, axis=-1, keepdims=True) + NORM_EPS)
    return (xf * inv * gain.astype(jnp.float32)).astype(x.dtype)


def retention_mixer(h, w_in, gn_gain, w_out):
    B, S, _ = h.shape
    H, DK, DV, C = RET_HEADS, RET_QK_DIM, RET_V_DIM, RET_CHUNK
    NC = S // C
    proj = h @ w_in
    q, k, v, g = jnp.split(proj, [H * DK, 2 * H * DK, 2 * H * DK + RET_VALUE_WIDTH], axis=-1)
    k = k * (DK ** -0.5)

    def to_chunks(t, d):
        return t.reshape(B, NC, C, H, d).transpose(1, 0, 3, 2, 4)

    qc, kc, vc = to_chunks(q, DK), to_chunks(k, DK), to_chunks(v, DV)

    log_g = jnp.log1p(-jnp.exp2(-5.0 - jnp.arange(H, dtype=jnp.float32)))
    pos = jnp.arange(C, dtype=jnp.float32)
    diff = pos[:, None] - pos[None, :]
    causal = diff >= 0
    decay_intra = jnp.where(causal, jnp.exp(log_g[:, None, None] * jnp.where(causal, diff, 0.0)), 0.0)
    q_decay = jnp.exp(log_g[:, None] * (pos + 1.0))[:, :, None]
    k_decay = jnp.exp(log_g[:, None] * (C - 1.0 - pos))[:, :, None]
    chunk_decay = jnp.exp(log_g * C)[:, None, None]
    dt = h.dtype
    decay_intra, q_decay, k_decay, chunk_decay = (a.astype(dt) for a in (decay_intra, q_decay, k_decay, chunk_decay))

    def step(state, inp):
        qb, kb, vb = inp
        scores = jnp.einsum('bhcd,bhmd->bhcm', qb, kb) * decay_intra
        o = jnp.einsum('bhcm,bhme->bhce', scores, vb) + jnp.einsum('bhcd,bhde->bhce', qb * q_decay, state)
        state = state * chunk_decay + jnp.einsum('bhcd,bhce->bhde', kb * k_decay, vb)
        return state, o

    state0 = jnp.zeros((B, H, DK, DV), dtype=dt)
    _, y = lax.scan(step, state0, (qc, kc, vc))
    y = y.transpose(1, 0, 3, 2, 4).reshape(B, S, H, DV)

    yf = y.astype(jnp.float32)
    mu = jnp.mean(yf, axis=-1, keepdims=True)
    var = jnp.mean(jnp.square(yf - mu), axis=-1, keepdims=True)
    yn = ((yf - mu) * lax.rsqrt(var + NORM_EPS) * gn_gain.astype(jnp.float32)).astype(dt)
    yn = yn.reshape(B, S, RET_VALUE_WIDTH)
    return (jax.nn.silu(g) * yn) @ w_out


def swa_mixer(h, w_in, sinks, w_out):
    B, S, _ = h.shape
    W, HKV, G, HD = SWA_WINDOW, SWA_KV_HEADS, SWA_GROUP, SWA_HEAD_DIM
    NB = S // W
    proj = h @ w_in
    q, k, v = jnp.split(proj, [SWA_Q_WIDTH, SWA_Q_WIDTH + SWA_KV_WIDTH], axis=-1)
    q = (q * (HD ** -0.5)).reshape(B, NB, W, HKV, G, HD)
    k = k.reshape(B, NB, W, HKV, HD)
    v = v.reshape(B, NB, W, HKV, HD)

    def band(t):
        prev = jnp.concatenate([jnp.zeros_like(t[:, :1]), t[:, :-1]], axis=1)
        return jnp.concatenate([prev, t], axis=2)

    kb_all = jnp.moveaxis(band(k), 1, 0)
    vb_all = jnp.moveaxis(band(v), 1, 0)
    qb_all = jnp.moveaxis(q, 1, 0)

    qi = jnp.arange(W)[:, None]
    kj = jnp.arange(2 * W)[None, :]
    rel = qi - kj + W
    in_window = (rel >= 0) & (rel < W)
    slopes = jnp.exp2(-8.0 * (jnp.arange(SWA_HEADS, dtype=jnp.float32) + 1.0) / SWA_HEADS).reshape(HKV, G)
    alibi = -slopes[:, :, None, None] * rel.astype(jnp.float32)
    sink = sinks.astype(jnp.float32).reshape(HKV, G)[None, :, :, None, None]

    def block_attn(args):
        qb, kb, vb, blk = args
        s = jnp.einsum('bqkgd,bskd->bkgqs', qb, kb).astype(jnp.float32) + alibi
        valid = in_window & ((kj >= W) | (blk > 0))
        s = jnp.where(valid, s, -jnp.inf)
        m = jnp.maximum(jnp.max(s, axis=-1, keepdims=True), sink)
        e = jnp.exp(s - m)
        probs = e / (jnp.sum(e, axis=-1, keepdims=True) + jnp.exp(sink - m))
        return jnp.einsum('bkgqs,bskd->bqkgd', probs.astype(vb.dtype), vb)

    o = lax.map(block_attn, (qb_all, kb_all, vb_all, jnp.arange(NB)))
    o = jnp.moveaxis(o, 0, 1).reshape(B, S, SWA_Q_WIDTH)
    return o @ w_out


def swiglu(h, w_in, w_out):
    gate, up = jnp.split(h @ w_in, 2, axis=-1)
    return (jax.nn.silu(gate) * up) @ w_out


def setup_inputs(seed: int = 0) -> dict:
    key = jax.random.key(seed)
    ks = jax.random.split(key, 16)
    f32 = jnp.float32

    def w(k, shape, fan_in, scale=1.0):
        return jax.random.normal(k, shape, f32) * (scale * fan_in ** -0.5)

    def gain(k, shape):
        return 1.0 + 0.02 * jax.random.normal(k, shape, f32)

    res_scale = (2.0 * DEPTH) ** -0.5
    return {
        "x": jax.random.normal(ks[0], (BATCH, SEQ, D_MODEL), f32),
        "p": jax.random.normal(ks[1], (DEPTH, BATCH, SEQ, PLE_DIM), f32),
        "norm_mix": gain(ks[2], (DEPTH, D_MODEL)),
        "norm_ffn": gain(ks[3], (DEPTH, D_MODEL)),
        "norm_ple": gain(ks[4], (DEPTH, D_MODEL)),
        "norm_final": gain(ks[5], (D_MODEL,)),
        "ret_w_in": w(ks[6], (N_RET_LAYERS, D_MODEL, RET_IN), D_MODEL),
        "ret_gn": gain(ks[7], (N_RET_LAYERS, RET_HEADS, RET_V_DIM)),
        "ret_w_out": w(ks[8], (N_RET_LAYERS, RET_VALUE_WIDTH, D_MODEL), RET_VALUE_WIDTH, res_scale),
        "swa_w_in": w(ks[9], (N_SWA_LAYERS, D_MODEL, SWA_IN), D_MODEL),
        "swa_sinks": 0.5 * jax.random.normal(ks[10], (N_SWA_LAYERS, SWA_HEADS), f32),
        "swa_w_out": w(ks[11], (N_SWA_LAYERS, SWA_Q_WIDTH, D_MODEL), SWA_Q_WIDTH, res_scale),
        "ffn_w_in": w(ks[12], (DEPTH, D_MODEL, 2 * FFN_HIDDEN), D_MODEL),
        "ffn_w_out": w(ks[13], (DEPTH, FFN_HIDDEN, D_MODEL), FFN_HIDDEN, res_scale),
        "ple_w_proj": w(ks[14], (DEPTH, PLE_DIM, D_MODEL), PLE_DIM, res_scale),
        "ple_w_gate": w(ks[15], (DEPTH, D_MODEL, D_MODEL), D_MODEL),
    }


def reference(x, p, norm_mix, norm_ffn, norm_ple, norm_final, ret_w_in, ret_gn, ret_w_out,
              swa_w_in, swa_sinks, swa_w_out, ffn_w_in, ffn_w_out, ple_w_proj, ple_w_gate):
    for i in range(DEPTH):
        h = rmsnorm(x, norm_mix[i])
        j = i // N_MIXERS
        if i % N_MIXERS == 0:
            x = x + retention_mixer(h, ret_w_in[j], ret_gn[j], ret_w_out[j])
        else:
            x = x + swa_mixer(h, swa_w_in[j], swa_sinks[j], swa_w_out[j])
        x = x + swiglu(rmsnorm(x, norm_ffn[i]), ffn_w_in[i], ffn_w_out[i])
        gate = jax.nn.sigmoid(rmsnorm(x, norm_ple[i]) @ ple_w_gate[i])
        x = x + gate * (p[i] @ ple_w_proj[i])
    return rmsnorm(x, norm_final)
```

```python
import functools

import jax
import jax.numpy as jnp
from jax import lax
from jax.experimental import pallas as pl
from jax.experimental.pallas import tpu as pltpu

F32 = jnp.float32
BF16 = jnp.bfloat16

NORM_EPS = 1e-6

RET_HEADS = 8
RET_CHUNK = 128
SWA_HEAD_DIM = 64
SWA_KV_HEADS = 4
SWA_WINDOW = 128

V7X_VMEM_BYTES = 64 * 1024 * 1024
VMEM_LIMIT_BYTES = V7X_VMEM_BYTES - 8 * 1024 * 1024
NORM_ROWS = 128


def _pick(n, prefs):
    for t in prefs:
        if n % t == 0:
            return t
    return n


def _params(*sem):
    return pltpu.CompilerParams(dimension_semantics=sem, vmem_limit_bytes=VMEM_LIMIT_BYTES)


def _rmsnorm_rows(x, gain):
    inv = lax.rsqrt(jnp.mean(x * x, axis=-1, keepdims=True) + NORM_EPS)
    return x * inv * gain


def _norm_into(x_ref, gain_ref, h_ref):
    tm = x_ref.shape[0]
    rows = min(NORM_ROWS, tm)

    def body(r, carry):
        sl = pl.ds(pl.multiple_of(r * rows, rows), rows)
        h_ref[sl, :] = _rmsnorm_rows(x_ref[sl, :], gain_ref[...]).astype(BF16)
        return carry

    lax.fori_loop(0, tm // rows, body, 0)


def _norm_matmul_kernel(x_ref, gain_ref, w_ref, o_ref, h_ref):
    @pl.when(pl.program_id(1) == 0)
    def _():
        _norm_into(x_ref, gain_ref, h_ref)

    o_ref[...] = jnp.dot(h_ref[...], w_ref[...], preferred_element_type=F32).astype(o_ref.dtype)


def norm_matmul(x, gain, w, layer, *, tm=None, tn=None):
    m, d = x.shape
    n = w.shape[-1]
    tm = tm or _pick(m, (1024, 512, 256, 128))
    tn = tn or _pick(n, (1024, 512, 256, 128))
    return pl.pallas_call(
        _norm_matmul_kernel,
        grid=(m // tm, n // tn),
        in_specs=[
            pl.BlockSpec((tm, d), lambda i, j: (i, 0)),
            pl.BlockSpec((1, d), lambda i, j: (0, 0)),
            pl.BlockSpec((None, d, tn), lambda i, j: (layer, 0, j)),
        ],
        out_specs=pl.BlockSpec((tm, tn), lambda i, j: (i, j)),
        out_shape=jax.ShapeDtypeStruct((m, n), BF16),
        scratch_shapes=[pltpu.VMEM((tm, d), BF16)],
        compiler_params=_params("parallel", "arbitrary"),
        name="norm_matmul",
    )(x, gain, w)


def _norm_swiglu_kernel(x_ref, gain_ref, wg_ref, wu_ref, o_ref, h_ref):
    @pl.when(pl.program_id(1) == 0)
    def _():
        _norm_into(x_ref, gain_ref, h_ref)

    h = h_ref[...]
    gate = jnp.dot(h, wg_ref[...], preferred_element_type=F32)
    up = jnp.dot(h, wu_ref[...], preferred_element_type=F32)
    o_ref[...] = (jax.nn.silu(gate) * up).astype(o_ref.dtype)


def norm_swiglu(x, gain, w_in, layer, *, tm=None, tn=None):
    m, d = x.shape
    hd = w_in.shape[-1] // 2
    tm = tm or _pick(m, (1024, 512, 256, 128))
    tn = tn or _pick(hd, (512, 256, 128))
    nh = hd // tn
    return pl.pallas_call(
        _norm_swiglu_kernel,
        grid=(m // tm, nh),
        in_specs=[
            pl.BlockSpec((tm, d), lambda i, j: (i, 0)),
            pl.BlockSpec((1, d), lambda i, j: (0, 0)),
            pl.BlockSpec((None, d, tn), lambda i, j: (layer, 0, j)),
            pl.BlockSpec((None, d, tn), lambda i, j: (layer, 0, j + nh)),
        ],
        out_specs=pl.BlockSpec((tm, tn), lambda i, j: (i, j)),
        out_shape=jax.ShapeDtypeStruct((m, hd), BF16),
        scratch_shapes=[pltpu.VMEM((tm, d), BF16)],
        compiler_params=_params("parallel", "arbitrary"),
        name="norm_swiglu",
    )(x, gain, w_in, w_in)


def _matmul_residual_kernel(a_ref, w_ref, x_ref, o_ref):
    o_ref[...] = x_ref[...] + jnp.dot(a_ref[...], w_ref[...], preferred_element_type=F32)


def matmul_residual(a, w, x, layer, *, tm=None, tn=None):
    m, k = a.shape
    n = w.shape[-1]
    tm = tm or _pick(m, (1024, 512, 256, 128))
    tn = tn or _pick(n, (512, 256, 128))
    return pl.pallas_call(
        _matmul_residual_kernel,
        grid=(m // tm, n // tn),
        in_specs=[
            pl.BlockSpec((tm, k), lambda i, j: (i, 0)),
            pl.BlockSpec((None, k, tn), lambda i, j: (layer, 0, j)),
            pl.BlockSpec((tm, tn), lambda i, j: (i, j)),
        ],
        out_specs=pl.BlockSpec((tm, tn), lambda i, j: (i, j)),
        out_shape=jax.ShapeDtypeStruct((m, n), F32),
        compiler_params=_params("parallel", "arbitrary"),
        name="matmul_residual",
    )(a, w, x)


def _ple_kernel(x_ref, gain_ref, wg_ref, p_ref, wp_ref, fgain_ref, o_ref, h_ref, *, final):
    _norm_into(x_ref, gain_ref, h_ref)
    gate = jax.nn.sigmoid(jnp.dot(h_ref[...], wg_ref[...], preferred_element_type=F32))
    pp = jnp.dot(p_ref[...].astype(BF16), wp_ref[...], preferred_element_type=F32)
    xn = x_ref[...] + gate * pp
    if final:
        xn = _rmsnorm_rows(xn, fgain_ref[...])
    o_ref[...] = xn


def ple_update(x, gain, w_gate, p, w_proj, final_gain, layer, *, final, tm=None):
    m, d = x.shape
    pd = p.shape[-1]
    tm = tm or _pick(m, (512, 256, 128))
    return pl.pallas_call(
        functools.partial(_ple_kernel, final=final),
        grid=(m // tm,),
        in_specs=[
            pl.BlockSpec((tm, d), lambda i: (i, 0)),
            pl.BlockSpec((1, d), lambda i: (0, 0)),
            pl.BlockSpec((None, d, d), lambda i: (layer, 0, 0)),
            pl.BlockSpec((None, tm, pd), lambda i: (layer, i, 0)),
            pl.BlockSpec((None, pd, d), lambda i: (layer, 0, 0)),
            pl.BlockSpec((1, d), lambda i: (0, 0)),
        ],
        out_specs=pl.BlockSpec((tm, d), lambda i: (i, 0)),
        out_shape=jax.ShapeDtypeStruct((m, d), F32),
        scratch_shapes=[pltpu.VMEM((tm, d), BF16)],
        compiler_params=_params("parallel"),
        name="ple_update",
    )(x, gain, w_gate, p, w_proj, final_gain)


def _ret_core_kernel(q_ref, k_ref, v_ref, g_ref, dint_ref, qd_ref, kd_ref, cd_ref, gn_ref,
                     o_ref, state_ref):
    c = RET_CHUNK

    @pl.when(pl.program_id(2) == 0)
    def _():
        state_ref[...] = jnp.zeros_like(state_ref)

    def body(ci, carry):
        sl = pl.ds(pl.multiple_of(ci * c, c), c)
        q = q_ref[sl, :]
        k = k_ref[sl, :]
        v = v_ref[sl, :]
        state = state_ref[...]
        scores = lax.dot_general(q, k, (((1,), (1,)), ((), ())), preferred_element_type=F32)
        scores = (scores * dint_ref[...]).astype(BF16)
        q_dec = (q.astype(F32) * qd_ref[...]).astype(BF16)
        o = (jnp.dot(scores, v, preferred_element_type=F32)
             + jnp.dot(q_dec, state.astype(BF16), preferred_element_type=F32))
        k_dec = (k.astype(F32) * kd_ref[...]).astype(BF16)
        state_ref[...] = state * cd_ref[...] + lax.dot_general(
            k_dec, v, (((0,), (0,)), ((), ())), preferred_element_type=F32)
        mu = jnp.mean(o, axis=-1, keepdims=True)
        dev = o - mu
        var = jnp.mean(dev * dev, axis=-1, keepdims=True)
        yn = dev * lax.rsqrt(var + NORM_EPS) * gn_ref[...]
        o_ref[sl, :] = (jax.nn.silu(g_ref[sl, :].astype(F32)) * yn).astype(o_ref.dtype)
        return carry

    lax.fori_loop(0, q_ref.shape[0] // c, body, 0)


def _retention_decays(dk, dv):
    h, c = RET_HEADS, RET_CHUNK
    log_g = jnp.log1p(-jnp.exp2(-5.0 - jnp.arange(h, dtype=F32)))
    pos = jnp.arange(c, dtype=F32)
    diff = pos[:, None] - pos[None, :]
    causal = diff >= 0
    decay_intra = jnp.where(causal, jnp.exp(log_g[:, None, None] * jnp.where(causal, diff, 0.0)), 0.0)
    q_decay = jnp.exp(log_g[:, None] * (pos + 1.0))[:, :, None]
    k_decay = jnp.exp(log_g[:, None] * (c - 1.0 - pos))[:, :, None]
    chunk_decay = jnp.exp(log_g * c)[:, None, None]
    k_scale = dk ** -0.5
    return (decay_intra * k_scale,
            jnp.broadcast_to(q_decay, (h, c, dk)),
            jnp.broadcast_to(k_decay * k_scale, (h, c, dk)),
            jnp.broadcast_to(chunk_decay, (h, 1, dv)))


def retention_core(proj, gn_gain, *, ts=None):
    b, s, width = proj.shape
    h = RET_HEADS
    dk = width // (6 * h)
    dv = 2 * dk
    ts = ts or _pick(s, (1024, 512, 256, 128))
    dint, qd, kd, cd = _retention_decays(dk, dv)
    head_table = lambda last: pl.BlockSpec((None,) + last, lambda bi, hi, ti: (hi, 0, 0))
    return pl.pallas_call(
        _ret_core_kernel,
        grid=(b, h, s // ts),
        in_specs=[
            pl.BlockSpec((None, ts, dk), lambda bi, hi, ti: (bi, ti, hi)),
            pl.BlockSpec((None, ts, dk), lambda bi, hi, ti: (bi, ti, h + hi)),
            pl.BlockSpec((None, ts, dv), lambda bi, hi, ti: (bi, ti, h + hi)),
            pl.BlockSpec((None, ts, dv), lambda bi, hi, ti: (bi, ti, 2 * h + hi)),
            head_table((RET_CHUNK, RET_CHUNK)),
            head_table((RET_CHUNK, dk)),
            head_table((RET_CHUNK, dk)),
            head_table((1, dv)),
            head_table((1, dv)),
        ],
        out_specs=pl.BlockSpec((None, ts, dv), lambda bi, hi, ti: (bi, ti, hi)),
        out_shape=jax.ShapeDtypeStruct((b, s, h * dv), BF16),
        scratch_shapes=[pltpu.VMEM((dk, dv), F32)],
        compiler_params=_params("parallel", "parallel", "arbitrary"),
        name="retention_core",
    )(proj, proj, proj, proj, dint, qd, kd, cd, gn_gain)


def _swa_core_kernel(slopes_ref, sinks_ref, q_ref, kp_ref, kc_ref, vp_ref, vc_ref, o_ref,
                     kb_ref, vb_ref, *, n_heads):
    w, hd = SWA_WINDOW, SWA_HEAD_DIM
    group = n_heads // SWA_KV_HEADS
    ts = q_ref.shape[0]
    kb_ref[0:w, :] = kp_ref[...]
    kb_ref[w:, :] = kc_ref[...]
    vb_ref[0:w, :] = vp_ref[...]
    vb_ref[w:, :] = vc_ref[...]

    qi = lax.broadcasted_iota(jnp.int32, (w, 2 * w), 0)
    kj = lax.broadcasted_iota(jnp.int32, (w, 2 * w), 1)
    rel = qi - kj + w
    in_window = (rel >= 0) & (rel < w)
    relf = rel.astype(F32)
    q_scale = hd ** -0.5
    first_tile = pl.program_id(1) == 0

    def body(n, carry):
        row0 = pl.multiple_of(n * w, w)
        has_prev = jnp.logical_not(jnp.logical_and(first_tile, n == 0))
        valid = in_window & ((kj >= w) | has_prev)
        q = q_ref[pl.ds(row0, w), :]
        kb = kb_ref[pl.ds(row0, 2 * w), :]
        vb = vb_ref[pl.ds(row0, 2 * w), :]
        outs = []
        for head in range(n_heads):
            kv = head // group
            qh = q[:, head * hd:(head + 1) * hd]
            kh = kb[:, kv * hd:(kv + 1) * hd]
            vh = vb[:, kv * hd:(kv + 1) * hd]
            s = lax.dot_general(qh, kh, (((1,), (1,)), ((), ())), preferred_element_type=F32)
            s = s * q_scale - slopes_ref[head] * relf
            s = jnp.where(valid, s, -jnp.inf)
            sink = sinks_ref[head]
            m = jnp.maximum(jnp.max(s, axis=-1, keepdims=True), sink)
            e = jnp.exp(s - m)
            denom = jnp.sum(e, axis=-1, keepdims=True) + jnp.exp(sink - m)
            probs = (e * (1.0 / denom)).astype(BF16)
            outs.append(jnp.dot(probs, vh, preferred_element_type=F32))
        o_ref[pl.ds(row0, w), :] = jnp.concatenate(outs, axis=1).astype(o_ref.dtype)
        return carry

    lax.fori_loop(0, ts // w, body, 0)


def swa_core(proj, sinks, *, ts=None):
    b, s, width = proj.shape
    w = SWA_WINDOW
    kvw = SWA_KV_HEADS * SWA_HEAD_DIM
    qw = width - 2 * kvw
    n_heads = qw // SWA_HEAD_DIM
    ts = ts or _pick(s, (512, 256, 128))
    bpt = ts // w
    slopes = jnp.exp2(-8.0 * (jnp.arange(n_heads, dtype=F32) + 1.0) / n_heads)
    prev = lambda col: (lambda bi, ti: (bi, jnp.maximum(ti * bpt - 1, 0), col))
    cur = lambda col: (lambda bi, ti: (bi, ti, col))
    smem = pl.BlockSpec(memory_space=pltpu.SMEM)
    return pl.pallas_call(
        functools.partial(_swa_core_kernel, n_heads=n_heads),
        grid=(b, s // ts),
        in_specs=[
            smem, smem,
            pl.BlockSpec((None, ts, qw), lambda bi, ti: (bi, ti, 0)),
            pl.BlockSpec((None, w, kvw), prev(qw // kvw)),
            pl.BlockSpec((None, ts, kvw), cur(qw // kvw)),
            pl.BlockSpec((None, w, kvw), prev(qw // kvw + 1)),
            pl.BlockSpec((None, ts, kvw), cur(qw // kvw + 1)),
        ],
        out_specs=pl.BlockSpec((None, ts, qw), lambda bi, ti: (bi, ti, 0)),
        out_shape=jax.ShapeDtypeStruct((b, s, qw), BF16),
        scratch_shapes=[pltpu.VMEM((ts + w, kvw), BF16), pltpu.VMEM((ts + w, kvw), BF16)],
        compiler_params=_params("parallel", "arbitrary"),
        name="swa_core",
    )(slopes, sinks, proj, proj, proj, proj, proj)


def kernel(x, p, norm_mix, norm_ffn, norm_ple, norm_final, ret_w_in, ret_gn, ret_w_out,
           swa_w_in, swa_sinks, swa_w_out, ffn_w_in, ffn_w_out, ple_w_proj, ple_w_gate):
    b, s, d = x.shape
    depth = p.shape[0]
    m = b * s
    ret_w_in, ret_w_out, swa_w_in, swa_w_out, ffn_w_in, ffn_w_out, ple_w_proj, ple_w_gate = (
        a.astype(BF16) for a in (ret_w_in, ret_w_out, swa_w_in, swa_w_out, ffn_w_in, ffn_w_out,
                                 ple_w_proj, ple_w_gate))
    p = p.reshape(depth, m, p.shape[-1])
    x = x.reshape(m, d)
    final_gain = norm_final[None, :]
    for i in range(depth):
        j = i // 2
        if i % 2 == 0:
            proj = norm_matmul(x, norm_mix[i][None, :], ret_w_in, j)
            y = retention_core(proj.reshape(b, s, -1), ret_gn[j][:, None, :])
            x = matmul_residual(y.reshape(m, -1), ret_w_out, x, j)
        else:
            proj = norm_matmul(x, norm_mix[i][None, :], swa_w_in, j)
            y = swa_core(proj.reshape(b, s, -1), swa_sinks[j])
            x = matmul_residual(y.reshape(m, -1), swa_w_out, x, j)
        hidden = norm_swiglu(x, norm_ffn[i][None, :], ffn_w_in, i)
        x = matmul_residual(hidden, ffn_w_out, x, i)
        x = ple_update(x, norm_ple[i][None, :], ple_w_gate, p, ple_w_proj, final_gain, i,
                       final=(i == depth - 1))
    return x.reshape(b, s, d)
```

```python
import functools

import jax
import jax.numpy as jnp
from jax import lax
from jax.experimental import pallas as pl
from jax.experimental.pallas import tpu as pltpu

F32 = jnp.float32
BF16 = jnp.bfloat16

NORM_EPS = 1e-6

RET_HEADS = 8
RET_CHUNK = 256
RET_HEADS_PER_STEP = 2
SWA_HEAD_DIM = 64
SWA_KV_HEADS = 4
SWA_WINDOW = 128
SWA_LOOKAHEAD = 2

V7X_VMEM_BYTES = 64 * 1024 * 1024
VMEM_LIMIT_BYTES = V7X_VMEM_BYTES - 8 * 1024 * 1024
NORM_ROWS = 128


def _pick(n, prefs):
    for t in prefs:
        if n % t == 0:
            return t
    return n


def _params(*sem):
    return pltpu.CompilerParams(dimension_semantics=sem, vmem_limit_bytes=VMEM_LIMIT_BYTES)


def _rmsnorm_rows(x, gain):
    inv = lax.rsqrt(jnp.mean(x * x, axis=-1, keepdims=True) + NORM_EPS)
    return x * inv * gain


def _norm_into(x_ref, gain_ref, h_ref):
    tm = x_ref.shape[0]
    rows = min(NORM_ROWS, tm)

    def body(r, carry):
        sl = pl.ds(pl.multiple_of(r * rows, rows), rows)
        h_ref[sl, :] = _rmsnorm_rows(x_ref[sl, :], gain_ref[...]).astype(BF16)
        return carry

    lax.fori_loop(0, tm // rows, body, 0)


def _norm_matmul_kernel(x_ref, gain_ref, w_ref, o_ref, h_ref):
    @pl.when(pl.program_id(1) == 0)
    def _():
        _norm_into(x_ref, gain_ref, h_ref)

    o_ref[...] = jnp.dot(h_ref[...], w_ref[...], preferred_element_type=F32).astype(o_ref.dtype)


def norm_matmul(x, gain, w, layer, *, tm=None, tn=None):
    m, d = x.shape
    n = w.shape[-1]
    tm = tm or _pick(m, (1024, 512, 256, 128))
    tn = tn or _pick(n, (1024, 512, 256, 128))
    return pl.pallas_call(
        _norm_matmul_kernel,
        grid=(m // tm, n // tn),
        in_specs=[
            pl.BlockSpec((tm, d), lambda i, j: (i, 0)),
            pl.BlockSpec((1, d), lambda i, j: (0, 0)),
            pl.BlockSpec((None, d, tn), lambda i, j: (layer, 0, j)),
        ],
        out_specs=pl.BlockSpec((tm, tn), lambda i, j: (i, j)),
        out_shape=jax.ShapeDtypeStruct((m, n), BF16),
        scratch_shapes=[pltpu.VMEM((tm, d), BF16)],
        compiler_params=_params("parallel", "arbitrary"),
        name="norm_matmul",
    )(x, gain, w)


def _norm_swiglu_kernel(x_ref, gain_ref, wg_ref, wu_ref, o_ref, h_ref):
    @pl.when(pl.program_id(1) == 0)
    def _():
        _norm_into(x_ref, gain_ref, h_ref)

    h = h_ref[...]
    gate = jnp.dot(h, wg_ref[...], preferred_element_type=F32)
    up = jnp.dot(h, wu_ref[...], preferred_element_type=F32)
    o_ref[...] = (jax.nn.silu(gate) * up).astype(o_ref.dtype)


def norm_swiglu(x, gain, w_in, layer, *, tm=None, tn=None):
    m, d = x.shape
    hd = w_in.shape[-1] // 2
    tm = tm or _pick(m, (1024, 512, 256, 128))
    tn = tn or _pick(hd, (512, 256, 128))
    nh = hd // tn
    return pl.pallas_call(
        _norm_swiglu_kernel,
        grid=(m // tm, nh),
        in_specs=[
            pl.BlockSpec((tm, d), lambda i, j: (i, 0)),
            pl.BlockSpec((1, d), lambda i, j: (0, 0)),
            pl.BlockSpec((None, d, tn), lambda i, j: (layer, 0, j)),
            pl.BlockSpec((None, d, tn), lambda i, j: (layer, 0, j + nh)),
        ],
        out_specs=pl.BlockSpec((tm, tn), lambda i, j: (i, j)),
        out_shape=jax.ShapeDtypeStruct((m, hd), BF16),
        scratch_shapes=[pltpu.VMEM((tm, d), BF16)],
        compiler_params=_params("parallel", "arbitrary"),
        name="norm_swiglu",
    )(x, gain, w_in, w_in)


def _matmul_residual_kernel(a_ref, w_ref, x_ref, o_ref):
    o_ref[...] = x_ref[...] + jnp.dot(a_ref[...], w_ref[...], preferred_element_type=F32)


def matmul_residual(a, w, x, layer, *, tm=None, tn=None):
    m, k = a.shape
    n = w.shape[-1]
    tm = tm or _pick(m, (1024, 512, 256, 128))
    tn = tn or _pick(n, (512, 256, 128))
    return pl.pallas_call(
        _matmul_residual_kernel,
        grid=(m // tm, n // tn),
        in_specs=[
            pl.BlockSpec((tm, k), lambda i, j: (i, 0)),
            pl.BlockSpec((None, k, tn), lambda i, j: (layer, 0, j)),
            pl.BlockSpec((tm, tn), lambda i, j: (i, j)),
        ],
        out_specs=pl.BlockSpec((tm, tn), lambda i, j: (i, j)),
        out_shape=jax.ShapeDtypeStruct((m, n), F32),
        compiler_params=_params("parallel", "arbitrary"),
        name="matmul_residual",
    )(a, w, x)


def _ple_kernel(x_ref, gain_ref, wg_ref, p_ref, wp_ref, fgain_ref, o_ref, h_ref, *, final):
    _norm_into(x_ref, gain_ref, h_ref)
    gate = jax.nn.sigmoid(jnp.dot(h_ref[...], wg_ref[...], preferred_element_type=F32))
    pp = jnp.dot(p_ref[...].astype(BF16), wp_ref[...], preferred_element_type=F32)
    xn = x_ref[...] + gate * pp
    if final:
        xn = _rmsnorm_rows(xn, fgain_ref[...])
    o_ref[...] = xn


def ple_update(x, gain, w_gate, p, w_proj, final_gain, layer, *, final, tm=None):
    m, d = x.shape
    pd = p.shape[-1]
    tm = tm or _pick(m, (512, 256, 128))
    return pl.pallas_call(
        functools.partial(_ple_kernel, final=final),
        grid=(m // tm,),
        in_specs=[
            pl.BlockSpec((tm, d), lambda i: (i, 0)),
            pl.BlockSpec((1, d), lambda i: (0, 0)),
            pl.BlockSpec((None, d, d), lambda i: (layer, 0, 0)),
            pl.BlockSpec((None, tm, pd), lambda i: (layer, i, 0)),
            pl.BlockSpec((None, pd, d), lambda i: (layer, 0, 0)),
            pl.BlockSpec((1, d), lambda i: (0, 0)),
        ],
        out_specs=pl.BlockSpec((tm, d), lambda i: (i, 0)),
        out_shape=jax.ShapeDtypeStruct((m, d), F32),
        scratch_shapes=[pltpu.VMEM((tm, d), BF16)],
        compiler_params=_params("parallel"),
        name="ple_update",
    )(x, gain, w_gate, p, w_proj, final_gain)


def _ret_core_kernel(q_ref, k_ref, v_ref, g_ref, dint_ref, qd_ref, kd_ref, cd_ref, gn_ref,
                     o_ref, state_ref, *, heads, chunk):
    dk = q_ref.shape[1] // heads
    dv = v_ref.shape[1] // heads

    @pl.when(pl.program_id(2) == 0)
    def _():
        state_ref[...] = jnp.zeros_like(state_ref)

    def body(ci, carry):
        sl = pl.ds(pl.multiple_of(ci * chunk, chunk), chunk)
        for hh in range(heads):
            ck = slice(hh * dk, (hh + 1) * dk)
            cv = slice(hh * dv, (hh + 1) * dv)
            q = q_ref[sl, ck]
            k = k_ref[sl, ck]
            v = v_ref[sl, cv]
            state = state_ref[hh]
            scores = lax.dot_general(q, k, (((1,), (1,)), ((), ())), preferred_element_type=F32)
            scores = (scores * dint_ref[hh]).astype(BF16)
            q_dec = (q.astype(F32) * qd_ref[hh]).astype(BF16)
            o = (jnp.dot(scores, v, preferred_element_type=F32)
                 + jnp.dot(q_dec, state.astype(BF16), preferred_element_type=F32))
            k_dec = (k.astype(F32) * kd_ref[hh]).astype(BF16)
            state_ref[hh] = state * cd_ref[hh] + lax.dot_general(
                k_dec, v, (((0,), (0,)), ((), ())), preferred_element_type=F32)
            mu = jnp.mean(o, axis=-1, keepdims=True)
            dev = o - mu
            var = jnp.mean(dev * dev, axis=-1, keepdims=True)
            yn = dev * lax.rsqrt(var + NORM_EPS) * gn_ref[hh]
            o_ref[sl, cv] = (jax.nn.silu(g_ref[sl, cv].astype(F32)) * yn).astype(o_ref.dtype)
        return carry

    lax.fori_loop(0, q_ref.shape[0] // chunk, body, 0)


def _retention_decays(c, dk, dv):
    h = RET_HEADS
    log_g = jnp.log1p(-jnp.exp2(-5.0 - jnp.arange(h, dtype=F32)))
    pos = jnp.arange(c, dtype=F32)
    diff = pos[:, None] - pos[None, :]
    causal = diff >= 0
    decay_intra = jnp.where(causal, jnp.exp(log_g[:, None, None] * jnp.where(causal, diff, 0.0)), 0.0)
    q_decay = jnp.exp(log_g[:, None] * (pos + 1.0))[:, :, None]
    k_decay = jnp.exp(log_g[:, None] * (c - 1.0 - pos))[:, :, None]
    chunk_decay = jnp.exp(log_g * c)[:, None, None]
    k_scale = dk ** -0.5
    return (decay_intra * k_scale,
            jnp.broadcast_to(q_decay, (h, c, dk)),
            jnp.broadcast_to(k_decay * k_scale, (h, c, dk)),
            jnp.broadcast_to(chunk_decay, (h, 1, dv)))


def retention_core(proj, gn_gain, *, ts=None, heads=RET_HEADS_PER_STEP, chunk=RET_CHUNK):
    b, s, width = proj.shape
    h = RET_HEADS
    dk = width // (6 * h)
    dv = 2 * dk
    ts = ts or _pick(s, (1024, 512, 256, 128))
    chunk = min(chunk, ts)
    hg = h // heads
    dint, qd, kd, cd = _retention_decays(chunk, dk, dv)
    head_table = lambda last: pl.BlockSpec((heads,) + last, lambda bi, hi, ti: (hi, 0, 0))
    return pl.pallas_call(
        functools.partial(_ret_core_kernel, heads=heads, chunk=chunk),
        grid=(b, hg, s // ts),
        in_specs=[
            pl.BlockSpec((None, ts, heads * dk), lambda bi, hi, ti: (bi, ti, hi)),
            pl.BlockSpec((None, ts, heads * dk), lambda bi, hi, ti: (bi, ti, hg + hi)),
            pl.BlockSpec((None, ts, heads * dv), lambda bi, hi, ti: (bi, ti, hg + hi)),
            pl.BlockSpec((None, ts, heads * dv), lambda bi, hi, ti: (bi, ti, 2 * hg + hi)),
            head_table((chunk, chunk)),
            head_table((chunk, dk)),
            head_table((chunk, dk)),
            head_table((1, dv)),
            head_table((1, dv)),
        ],
        out_specs=pl.BlockSpec((None, ts, heads * dv), lambda bi, hi, ti: (bi, ti, hi)),
        out_shape=jax.ShapeDtypeStruct((b, s, h * dv), BF16),
        scratch_shapes=[pltpu.VMEM((heads, dk, dv), F32)],
        compiler_params=_params("parallel", "parallel", "arbitrary"),
        name="retention_core",
    )(proj, proj, proj, proj, dint, qd, kd, cd, gn_gain)


def _swa_core_kernel(sinks_ref, q_ref, kp_ref, kc_ref, vp_ref, vc_ref, bias_ref, o_ref,
                     kb_ref, vb_ref, *, n_heads):
    w, hd = SWA_WINDOW, SWA_HEAD_DIM
    group = n_heads // SWA_KV_HEADS
    ts = q_ref.shape[0]
    q_scale = hd ** -0.5
    kb_ref[0:w, :] = kp_ref[...] * q_scale
    kb_ref[w:, :] = kc_ref[...] * q_scale
    vb_ref[0:w, :] = vp_ref[...]
    vb_ref[w:, :] = vc_ref[...]

    row = lax.broadcasted_iota(jnp.int32, (w, w), 0)
    lane = lax.broadcasted_iota(jnp.int32, (w, w), 1)
    upper = lane > row
    left = lane < hd
    zero = jnp.zeros((w, w), BF16)
    upper2 = jnp.concatenate([upper, upper], axis=1)
    zero2 = jnp.zeros((w, 2 * w), BF16)
    ones_diag = jnp.concatenate([jnp.where(left, 1.0, 0.0), jnp.where(left, 0.0, 1.0)],
                                axis=0).astype(BF16)
    first_tile = pl.program_id(1) == 0
    nt = (((1,), (1,)), ((), ()))

    def split(slab):
        swapped = pltpu.roll(slab, hd, axis=1)
        return ((jnp.where(left, slab, zero), jnp.where(left, zero, swapped)),
                (jnp.where(left, swapped, zero), jnp.where(left, zero, slab)))

    def body(n, carry):
        row0 = pl.multiple_of(n * w, w)
        fidx = jnp.logical_and(first_tile, n == 0).astype(jnp.int32)
        rows = pl.ds(row0, w)
        prev_rows = pl.ds(row0, w)
        cur_rows = pl.ds(row0 + w, w)
        kprev, kcur, vprev, vcur = [], [], [], []
        for slab_i in range(SWA_KV_HEADS // 2):
            cols = slice(slab_i * 2 * hd, (slab_i + 1) * 2 * hd)
            kprev.extend(split(kb_ref[prev_rows, cols]))
            kcur.extend(split(kb_ref[cur_rows, cols]))
            vprev.extend(jnp.concatenate([jnp.concatenate(t, axis=0), ones_diag], axis=1)
                         for t in split(vb_ref[prev_rows, cols]))
            vcur.extend(jnp.concatenate([jnp.concatenate(t, axis=0), ones_diag], axis=1)
                        for t in split(vb_ref[cur_rows, cols]))

        def scores(pair):
            kv = (2 * pair) // group
            qp = q_ref[rows, 2 * pair * hd:(2 * pair + 2) * hd]
            out = []
            for side in range(2):
                sp = lax.dot_general(qp, kprev[kv][side], nt, preferred_element_type=F32)
                sc = lax.dot_general(qp, kcur[kv][side], nt, preferred_element_type=F32)
                out.append(jnp.where(upper, sp, sc) + bias_ref[fidx, 2 * pair + side])
            return out

        def softmax_numerators(pair, s_pair):
            out = []
            for side in range(2):
                sink = sinks_ref[2 * pair + side]
                m = jnp.maximum(jnp.max(s_pair[side], axis=-1, keepdims=True), sink)
                out.append((jnp.exp(s_pair[side] - m).astype(BF16), jnp.exp(sink - m)))
            return out

        def weighted_values(pair, e_pair):
            kv = (2 * pair) // group
            e = jnp.concatenate([e_pair[0][0], e_pair[1][0]], axis=1)
            both = (jnp.dot(jnp.where(upper2, e, zero2), vprev[kv], preferred_element_type=F32)
                    + jnp.dot(jnp.where(upper2, zero2, e), vcur[kv], preferred_element_type=F32))
            acc = both[:, :w]
            den = both[:, w:] + jnp.where(left, e_pair[0][1], e_pair[1][1])
            o_ref[rows, 2 * pair * hd:(2 * pair + 2) * hd] = (acc * (1.0 / den)).astype(o_ref.dtype)

        n_pairs = n_heads // 2
        s_tiles, e_tiles = {}, {}
        for t in range(n_pairs + 2 * SWA_LOOKAHEAD):
            if t < n_pairs:
                s_tiles[t] = scores(t)
            t1 = t - SWA_LOOKAHEAD
            if 0 <= t1 < n_pairs:
                e_tiles[t1] = softmax_numerators(t1, s_tiles.pop(t1))
            t2 = t1 - SWA_LOOKAHEAD
            if 0 <= t2 < n_pairs:
                weighted_values(t2, e_tiles.pop(t2))
        return carry

    lax.fori_loop(0, ts // w, body, 0)


def _swa_bias(n_heads):
    w = SWA_WINDOW
    slopes = jnp.exp2(-8.0 * (jnp.arange(n_heads, dtype=F32) + 1.0) / n_heads)
    qi = jnp.arange(w)[:, None]
    kc = jnp.arange(w)[None, :]
    upper = kc > qi
    rel = jnp.where(upper, qi - kc + w, qi - kc).astype(F32)
    bias = -slopes[:, None, None] * rel
    return jnp.stack([bias, jnp.where(upper, -jnp.inf, bias)])


def swa_core(proj, sinks, *, ts=None):
    b, s, width = proj.shape
    w = SWA_WINDOW
    kvw = SWA_KV_HEADS * SWA_HEAD_DIM
    qw = width - 2 * kvw
    n_heads = qw // SWA_HEAD_DIM
    ts = ts or _pick(s, (512, 256, 128))
    bpt = ts // w
    prev = lambda col: (lambda bi, ti: (bi, jnp.maximum(ti * bpt - 1, 0), col))
    cur = lambda col: (lambda bi, ti: (bi, ti, col))
    return pl.pallas_call(
        functools.partial(_swa_core_kernel, n_heads=n_heads),
        grid=(b, s // ts),
        in_specs=[
            pl.BlockSpec(memory_space=pltpu.SMEM),
            pl.BlockSpec((None, ts, qw), lambda bi, ti: (bi, ti, 0)),
            pl.BlockSpec((None, w, kvw), prev(qw // kvw)),
            pl.BlockSpec((None, ts, kvw), cur(qw // kvw)),
            pl.BlockSpec((None, w, kvw), prev(qw // kvw + 1)),
            pl.BlockSpec((None, ts, kvw), cur(qw // kvw + 1)),
            pl.BlockSpec((2, n_heads, w, w), lambda bi, ti: (0, 0, 0, 0)),
        ],
        out_specs=pl.BlockSpec((None, ts, qw), lambda bi, ti: (bi, ti, 0)),
        out_shape=jax.ShapeDtypeStruct((b, s, qw), BF16),
        scratch_shapes=[pltpu.VMEM((ts + w, kvw), BF16), pltpu.VMEM((ts + w, kvw), BF16)],
        compiler_params=_params("parallel", "arbitrary"),
        name="swa_core",
    )(sinks, proj, proj, proj, proj, proj, _swa_bias(n_heads))


def kernel(x, p, norm_mix, norm_ffn, norm_ple, norm_final, ret_w_in, ret_gn, ret_w_out,
           swa_w_in, swa_sinks, swa_w_out, ffn_w_in, ffn_w_out, ple_w_proj, ple_w_gate):
    b, s, d = x.shape
    depth = p.shape[0]
    m = b * s
    ret_w_in, ret_w_out, swa_w_in, swa_w_out, ffn_w_in, ffn_w_out, ple_w_proj, ple_w_gate = (
        a.astype(BF16) for a in (ret_w_in, ret_w_out, swa_w_in, swa_w_out, ffn_w_in, ffn_w_out,
                                 ple_w_proj, ple_w_gate))
    p = p.reshape(depth, m, p.shape[-1])
    x = x.reshape(m, d)
    final_gain = norm_final[None, :]
    for i in range(depth):
        j = i // 2
        if i % 2 == 0:
            proj = norm_matmul(x, norm_mix[i][None, :], ret_w_in, j)
            y = retention_core(proj.reshape(b, s, -1), ret_gn[j][:, None, :])
            x = matmul_residual(y.reshape(m, -1), ret_w_out, x, j)
        else:
            proj = norm_matmul(x, norm_mix[i][None, :], swa_w_in, j)
            y = swa_core(proj.reshape(b, s, -1), swa_sinks[j])
            x = matmul_residual(y.reshape(m, -1), swa_w_out, x, j)
        hidden = norm_swiglu(x, norm_ffn[i][None, :], ffn_w_in, i)
        x = matmul_residual(hidden, ffn_w_out, x, i)
        x = ple_update(x, norm_ple[i][None, :], ple_w_gate, p, ple_w_proj, final_gain, i,
                       final=(i == depth - 1))
    return x.reshape(b, s, d)
```

```python
import functools

import jax
import jax.numpy as jnp
from jax import lax
from jax.experimental import pallas as pl
from jax.experimental.pallas import tpu as pltpu

F32 = jnp.float32
BF16 = jnp.bfloat16

NORM_EPS = 1e-6

RET_HEADS = 8
RET_CHUNK = 256
RET_HEADS_PER_STEP = 2
SWA_HEAD_DIM = 64
SWA_KV_HEADS = 4
SWA_WINDOW = 128
SWA_LOOKAHEAD = 2

V7X_VMEM_BYTES = 64 * 1024 * 1024
VMEM_TILE_BUDGET = 54 * 1024 * 1024
VMEM_HEADROOM = 6 * 1024 * 1024
VMEM_LIMIT_BYTES = V7X_VMEM_BYTES - 8 * 1024 * 1024
LANES = 128
NORM_ROWS = 128
ROW_TILES = (1024, 512, 256, 128)


def _pick(n, prefs):
    for t in prefs:
        if n % t == 0:
            return t
    return n


def _params(*sem, vmem_bytes=None):
    limit = VMEM_LIMIT_BYTES if vmem_bytes is None else min(vmem_bytes + VMEM_HEADROOM,
                                                            V7X_VMEM_BYTES - 2 * 1024 * 1024)
    return pltpu.CompilerParams(dimension_semantics=sem, vmem_limit_bytes=limit)


def _choose_tiles(m, n, vmem_bytes):
    best = None
    for tm in ROW_TILES:
        if m % tm:
            continue
        for parts in range(1, n // LANES + 1):
            tn = n // parts
            if n % parts or tn % LANES:
                continue
            need = vmem_bytes(tm, tn)
            if need > VMEM_TILE_BUDGET:
                continue
            key = (tm * tn, tn == n, tm)
            if best is None or key > best[0]:
                best = (key, tm, tn, need)
    if best is None:
        raise ValueError("no matmul tile fits VMEM")
    return best[1:]


def _weight_spec(block, index_map, resident):
    if resident:
        return pl.BlockSpec(block, index_map, pipeline_mode=pl.Buffered(1))
    return pl.BlockSpec(block, index_map)


def _rmsnorm_rows(x, gain):
    inv = lax.rsqrt(jnp.mean(x * x, axis=-1, keepdims=True) + NORM_EPS)
    return x * inv * gain


def _norm_into(x_ref, gain_ref, h_ref):
    tm = x_ref.shape[0]
    rows = min(NORM_ROWS, tm)

    def body(r, carry):
        sl = pl.ds(pl.multiple_of(r * rows, rows), rows)
        h_ref[sl, :] = _rmsnorm_rows(x_ref[sl, :], gain_ref[...]).astype(BF16)
        return carry

    lax.fori_loop(0, tm // rows, body, 0)


def _norm_matmul_kernel(x_ref, gain_ref, w_ref, o_ref, h_ref):
    @pl.when(pl.program_id(1) == 0)
    def _():
        _norm_into(x_ref, gain_ref, h_ref)

    o_ref[...] = jnp.dot(h_ref[...], w_ref[...], preferred_element_type=F32).astype(o_ref.dtype)


def norm_matmul(x, gain, w, layer, *, tm=None, tn=None):
    m, d = x.shape
    n = w.shape[-1]

    def vmem_bytes(tm, tn):
        w_bufs = 1 if tn == n else 2
        return 2 * tm * d * 4 + tm * d * 2 + w_bufs * d * tn * 2 + 2 * tm * tn * 2 + tm * tn * 4

    if tm is None or tn is None:
        tm, tn, need = _choose_tiles(m, n, vmem_bytes)
    else:
        need = vmem_bytes(tm, tn)
    return pl.pallas_call(
        _norm_matmul_kernel,
        grid=(m // tm, n // tn),
        in_specs=[
            pl.BlockSpec((tm, d), lambda i, j: (i, 0)),
            pl.BlockSpec((1, d), lambda i, j: (0, 0)),
            _weight_spec((None, d, tn), lambda i, j: (layer, 0, j), tn == n),
        ],
        out_specs=pl.BlockSpec((tm, tn), lambda i, j: (i, j)),
        out_shape=jax.ShapeDtypeStruct((m, n), BF16),
        scratch_shapes=[pltpu.VMEM((tm, d), BF16)],
        compiler_params=_params("parallel", "arbitrary", vmem_bytes=need),
        name="norm_matmul",
    )(x, gain, w)


def _norm_swiglu_kernel(x_ref, gain_ref, wg_ref, wu_ref, o_ref, h_ref):
    @pl.when(pl.program_id(1) == 0)
    def _():
        _norm_into(x_ref, gain_ref, h_ref)

    h = h_ref[...]
    gate = jnp.dot(h, wg_ref[...], preferred_element_type=F32)
    up = jnp.dot(h, wu_ref[...], preferred_element_type=F32)
    o_ref[...] = (jax.nn.silu(gate) * up).astype(o_ref.dtype)


def norm_swiglu(x, gain, w_in, layer, *, tm=None, tn=None):
    m, d = x.shape
    hd = w_in.shape[-1] // 2

    def vmem_bytes(tm, tn):
        return 2 * tm * d * 4 + tm * d * 2 + 2 * 2 * d * tn * 2 + 2 * tm * tn * 2 + 2 * tm * tn * 4

    if tm is None or tn is None:
        tm, tn, need = _choose_tiles(m, hd, vmem_bytes)
    else:
        need = vmem_bytes(tm, tn)
    nh = hd // tn
    return pl.pallas_call(
        _norm_swiglu_kernel,
        grid=(m // tm, nh),
        in_specs=[
            pl.BlockSpec((tm, d), lambda i, j: (i, 0)),
            pl.BlockSpec((1, d), lambda i, j: (0, 0)),
            pl.BlockSpec((None, d, tn), lambda i, j: (layer, 0, j)),
            pl.BlockSpec((None, d, tn), lambda i, j: (layer, 0, j + nh)),
        ],
        out_specs=pl.BlockSpec((tm, tn), lambda i, j: (i, j)),
        out_shape=jax.ShapeDtypeStruct((m, hd), BF16),
        scratch_shapes=[pltpu.VMEM((tm, d), BF16)],
        compiler_params=_params("parallel", "arbitrary", vmem_bytes=need),
        name="norm_swiglu",
    )(x, gain, w_in, w_in)


def _matmul_residual_kernel(a_ref, w_ref, x_ref, o_ref):
    o_ref[...] = x_ref[...] + jnp.dot(a_ref[...], w_ref[...], preferred_element_type=F32)


def matmul_residual(a, w, x, layer, *, tm=None, tn=None):
    m, k = a.shape
    n = w.shape[-1]

    def vmem_bytes(tm, tn):
        w_bufs = 1 if tn == n else 2
        return 2 * tm * k * 2 + w_bufs * k * tn * 2 + 2 * 2 * tm * tn * 4 + tm * tn * 4

    if tm is None or tn is None:
        tm, tn, need = _choose_tiles(m, n, vmem_bytes)
    else:
        need = vmem_bytes(tm, tn)
    return pl.pallas_call(
        _matmul_residual_kernel,
        grid=(m // tm, n // tn),
        in_specs=[
            pl.BlockSpec((tm, k), lambda i, j: (i, 0)),
            _weight_spec((None, k, tn), lambda i, j: (layer, 0, j), tn == n),
            pl.BlockSpec((tm, tn), lambda i, j: (i, j)),
        ],
        out_specs=pl.BlockSpec((tm, tn), lambda i, j: (i, j)),
        out_shape=jax.ShapeDtypeStruct((m, n), F32),
        compiler_params=_params("parallel", "arbitrary", vmem_bytes=need),
        name="matmul_residual",
    )(a, w, x)


def _ple_kernel(x_ref, gain_ref, wg_ref, p_ref, wp_ref, fgain_ref, o_ref, h_ref, *, final):
    _norm_into(x_ref, gain_ref, h_ref)
    gate = jax.nn.sigmoid(jnp.dot(h_ref[...], wg_ref[...], preferred_element_type=F32))
    pp = jnp.dot(p_ref[...].astype(BF16), wp_ref[...], preferred_element_type=F32)
    xn = x_ref[...] + gate * pp
    if final:
        xn = _rmsnorm_rows(xn, fgain_ref[...])
    o_ref[...] = xn


def ple_update(x, gain, w_gate, p, w_proj, final_gain, layer, *, final, tm=None):
    m, d = x.shape
    pd = p.shape[-1]
    tm = tm or _pick(m, (512, 256, 128))
    return pl.pallas_call(
        functools.partial(_ple_kernel, final=final),
        grid=(m // tm,),
        in_specs=[
            pl.BlockSpec((tm, d), lambda i: (i, 0)),
            pl.BlockSpec((1, d), lambda i: (0, 0)),
            pl.BlockSpec((None, d, d), lambda i: (layer, 0, 0)),
            pl.BlockSpec((None, tm, pd), lambda i: (layer, i, 0)),
            pl.BlockSpec((None, pd, d), lambda i: (layer, 0, 0)),
            pl.BlockSpec((1, d), lambda i: (0, 0)),
        ],
        out_specs=pl.BlockSpec((tm, d), lambda i: (i, 0)),
        out_shape=jax.ShapeDtypeStruct((m, d), F32),
        scratch_shapes=[pltpu.VMEM((tm, d), BF16)],
        compiler_params=_params("parallel"),
        name="ple_update",
    )(x, gain, w_gate, p, w_proj, final_gain)


def _ret_core_kernel(q_ref, k_ref, v_ref, g_ref, dint_ref, qd_ref, kd_ref, cd_ref, gn_ref,
                     o_ref, state_ref, *, heads, chunk):
    dk = q_ref.shape[1] // heads
    dv = v_ref.shape[1] // heads

    @pl.when(pl.program_id(2) == 0)
    def _():
        state_ref[...] = jnp.zeros_like(state_ref)

    def body(ci, carry):
        sl = pl.ds(pl.multiple_of(ci * chunk, chunk), chunk)
        for hh in range(heads):
            ck = slice(hh * dk, (hh + 1) * dk)
            cv = slice(hh * dv, (hh + 1) * dv)
            q = q_ref[sl, ck]
            k = k_ref[sl, ck]
            v = v_ref[sl, cv]
            state = state_ref[hh]
            scores = lax.dot_general(q, k, (((1,), (1,)), ((), ())), preferred_element_type=F32)
            scores = (scores * dint_ref[hh]).astype(BF16)
            q_dec = (q.astype(F32) * qd_ref[hh]).astype(BF16)
            o = (jnp.dot(scores, v, preferred_element_type=F32)
                 + jnp.dot(q_dec, state.astype(BF16), preferred_element_type=F32))
            k_dec = (k.astype(F32) * kd_ref[hh]).astype(BF16)
            state_ref[hh] = state * cd_ref[hh] + lax.dot_general(
                k_dec, v, (((0,), (0,)), ((), ())), preferred_element_type=F32)
            mu = jnp.mean(o, axis=-1, keepdims=True)
            dev = o - mu
            var = jnp.mean(dev * dev, axis=-1, keepdims=True)
            yn = dev * lax.rsqrt(var + NORM_EPS) * gn_ref[hh]
            o_ref[sl, cv] = (jax.nn.silu(g_ref[sl, cv].astype(F32)) * yn).astype(o_ref.dtype)
        return carry

    lax.fori_loop(0, q_ref.shape[0] // chunk, body, 0)


def _retention_decays(c, dk, dv):
    h = RET_HEADS
    log_g = jnp.log1p(-jnp.exp2(-5.0 - jnp.arange(h, dtype=F32)))
    pos = jnp.arange(c, dtype=F32)
    diff = pos[:, None] - pos[None, :]
    causal = diff >= 0
    decay_intra = jnp.where(causal, jnp.exp(log_g[:, None, None] * jnp.where(causal, diff, 0.0)), 0.0)
    q_decay = jnp.exp(log_g[:, None] * (pos + 1.0))[:, :, None]
    k_decay = jnp.exp(log_g[:, None] * (c - 1.0 - pos))[:, :, None]
    chunk_decay = jnp.exp(log_g * c)[:, None, None]
    k_scale = dk ** -0.5
    return (decay_intra * k_scale,
            jnp.broadcast_to(q_decay, (h, c, dk)),
            jnp.broadcast_to(k_decay * k_scale, (h, c, dk)),
            jnp.broadcast_to(chunk_decay, (h, 1, dv)))


def retention_core(proj, gn_gain, *, ts=None, heads=RET_HEADS_PER_STEP, chunk=RET_CHUNK):
    b, s, width = proj.shape
    h = RET_HEADS
    dk = width // (6 * h)
    dv = 2 * dk
    ts = ts or _pick(s, (1024, 512, 256, 128))
    chunk = min(chunk, ts)
    hg = h // heads
    dint, qd, kd, cd = _retention_decays(chunk, dk, dv)
    head_table = lambda last: pl.BlockSpec((heads,) + last, lambda bi, hi, ti: (hi, 0, 0))
    return pl.pallas_call(
        functools.partial(_ret_core_kernel, heads=heads, chunk=chunk),
        grid=(b, hg, s // ts),
        in_specs=[
            pl.BlockSpec((None, ts, heads * dk), lambda bi, hi, ti: (bi, ti, hi)),
            pl.BlockSpec((None, ts, heads * dk), lambda bi, hi, ti: (bi, ti, hg + hi)),
            pl.BlockSpec((None, ts, heads * dv), lambda bi, hi, ti: (bi, ti, hg + hi)),
            pl.BlockSpec((None, ts, heads * dv), lambda bi, hi, ti: (bi, ti, 2 * hg + hi)),
            head_table((chunk, chunk)),
            head_table((chunk, dk)),
            head_table((chunk, dk)),
            head_table((1, dv)),
            head_table((1, dv)),
        ],
        out_specs=pl.BlockSpec((None, ts, heads * dv), lambda bi, hi, ti: (bi, ti, hi)),
        out_shape=jax.ShapeDtypeStruct((b, s, h * dv), BF16),
        scratch_shapes=[pltpu.VMEM((heads, dk, dv), F32)],
        compiler_params=_params("parallel", "parallel", "arbitrary"),
        name="retention_core",
    )(proj, proj, proj, proj, dint, qd, kd, cd, gn_gain)


def _swa_core_kernel(sinks_ref, q_ref, kp_ref, kc_ref, vp_ref, vc_ref, bias_ref, o_ref,
                     kb_ref, vb_ref, *, n_heads):
    w, hd = SWA_WINDOW, SWA_HEAD_DIM
    group = n_heads // SWA_KV_HEADS
    ts = q_ref.shape[0]
    q_scale = hd ** -0.5
    kb_ref[0:w, :] = kp_ref[...] * q_scale
    kb_ref[w:, :] = kc_ref[...] * q_scale
    vb_ref[0:w, :] = vp_ref[...]
    vb_ref[w:, :] = vc_ref[...]

    row = lax.broadcasted_iota(jnp.int32, (w, w), 0)
    lane = lax.broadcasted_iota(jnp.int32, (w, w), 1)
    upper = lane > row
    left = lane < hd
    zero = jnp.zeros((w, w), BF16)
    upper2 = jnp.concatenate([upper, upper], axis=1)
    zero2 = jnp.zeros((w, 2 * w), BF16)
    ones_diag = jnp.concatenate([jnp.where(left, 1.0, 0.0), jnp.where(left, 0.0, 1.0)],
                                axis=0).astype(BF16)
    first_tile = pl.program_id(1) == 0
    nt = (((1,), (1,)), ((), ()))

    def split(slab):
        swapped = pltpu.roll(slab, hd, axis=1)
        return ((jnp.where(left, slab, zero), jnp.where(left, zero, swapped)),
                (jnp.where(left, swapped, zero), jnp.where(left, zero, slab)))

    def body(n, carry):
        row0 = pl.multiple_of(n * w, w)
        fidx = jnp.logical_and(first_tile, n == 0).astype(jnp.int32)
        rows = pl.ds(row0, w)
        prev_rows = pl.ds(row0, w)
        cur_rows = pl.ds(row0 + w, w)
        kprev, kcur, vprev, vcur = [], [], [], []
        for slab_i in range(SWA_KV_HEADS // 2):
            cols = slice(slab_i * 2 * hd, (slab_i + 1) * 2 * hd)
            kprev.extend(split(kb_ref[prev_rows, cols]))
            kcur.extend(split(kb_ref[cur_rows, cols]))
            vprev.extend(jnp.concatenate([jnp.concatenate(t, axis=0), ones_diag], axis=1)
                         for t in split(vb_ref[prev_rows, cols]))
            vcur.extend(jnp.concatenate([jnp.concatenate(t, axis=0), ones_diag], axis=1)
                        for t in split(vb_ref[cur_rows, cols]))

        def scores(pair):
            kv = (2 * pair) // group
            qp = q_ref[rows, 2 * pair * hd:(2 * pair + 2) * hd]
            out = []
            for side in range(2):
                sp = lax.dot_general(qp, kprev[kv][side], nt, preferred_element_type=F32)
                sc = lax.dot_general(qp, kcur[kv][side], nt, preferred_element_type=F32)
                out.append(jnp.where(upper, sp, sc) + bias_ref[fidx, 2 * pair + side])
            return out

        def softmax_numerators(pair, s_pair):
            out = []
            for side in range(2):
                sink = sinks_ref[2 * pair + side]
                m = jnp.maximum(jnp.max(s_pair[side], axis=-1, keepdims=True), sink)
                out.append((jnp.exp(s_pair[side] - m).astype(BF16), jnp.exp(sink - m)))
            return out

        def weighted_values(pair, e_pair):
            kv = (2 * pair) // group
            e = jnp.concatenate([e_pair[0][0], e_pair[1][0]], axis=1)
            both = (jnp.dot(jnp.where(upper2, e, zero2), vprev[kv], preferred_element_type=F32)
                    + jnp.dot(jnp.where(upper2, zero2, e), vcur[kv], preferred_element_type=F32))
            acc = both[:, :w]
            den = both[:, w:] + jnp.where(left, e_pair[0][1], e_pair[1][1])
            o_ref[rows, 2 * pair * hd:(2 * pair + 2) * hd] = (acc * (1.0 / den)).astype(o_ref.dtype)

        n_pairs = n_heads // 2
        s_tiles, e_tiles = {}, {}
        for t in range(n_pairs + 2 * SWA_LOOKAHEAD):
            if t < n_pairs:
                s_tiles[t] = scores(t)
            t1 = t - SWA_LOOKAHEAD
            if 0 <= t1 < n_pairs:
                e_tiles[t1] = softmax_numerators(t1, s_tiles.pop(t1))
            t2 = t1 - SWA_LOOKAHEAD
            if 0 <= t2 < n_pairs:
                weighted_values(t2, e_tiles.pop(t2))
        return carry

    lax.fori_loop(0, ts // w, body, 0)


def _swa_bias(n_heads):
    w = SWA_WINDOW
    slopes = jnp.exp2(-8.0 * (jnp.arange(n_heads, dtype=F32) + 1.0) / n_heads)
    qi = jnp.arange(w)[:, None]
    kc = jnp.arange(w)[None, :]
    upper = kc > qi
    rel = jnp.where(upper, qi - kc + w, qi - kc).astype(F32)
    bias = -slopes[:, None, None] * rel
    return jnp.stack([bias, jnp.where(upper, -jnp.inf, bias)])


def swa_core(proj, sinks, *, ts=None):
    b, s, width = proj.shape
    w = SWA_WINDOW
    kvw = SWA_KV_HEADS * SWA_HEAD_DIM
    qw = width - 2 * kvw
    n_heads = qw // SWA_HEAD_DIM
    ts = ts or _pick(s, (512, 256, 128))
    bpt = ts // w
    prev = lambda col: (lambda bi, ti: (bi, jnp.maximum(ti * bpt - 1, 0), col))
    cur = lambda col: (lambda bi, ti: (bi, ti, col))
    return pl.pallas_call(
        functools.partial(_swa_core_kernel, n_heads=n_heads),
        grid=(b, s // ts),
        in_specs=[
            pl.BlockSpec(memory_space=pltpu.SMEM),
            pl.BlockSpec((None, ts, qw), lambda bi, ti: (bi, ti, 0)),
            pl.BlockSpec((None, w, kvw), prev(qw // kvw)),
            pl.BlockSpec((None, ts, kvw), cur(qw // kvw)),
            pl.BlockSpec((None, w, kvw), prev(qw // kvw + 1)),
            pl.BlockSpec((None, ts, kvw), cur(qw // kvw + 1)),
            pl.BlockSpec((2, n_heads, w, w), lambda bi, ti: (0, 0, 0, 0)),
        ],
        out_specs=pl.BlockSpec((None, ts, qw), lambda bi, ti: (bi, ti, 0)),
        out_shape=jax.ShapeDtypeStruct((b, s, qw), BF16),
        scratch_shapes=[pltpu.VMEM((ts + w, kvw), BF16), pltpu.VMEM((ts + w, kvw), BF16)],
        compiler_params=_params("parallel", "arbitrary"),
        name="swa_core",
    )(sinks, proj, proj, proj, proj, proj, _swa_bias(n_heads))


def kernel(x, p, norm_mix, norm_ffn, norm_ple, norm_final, ret_w_in, ret_gn, ret_w_out,
           swa_w_in, swa_sinks, swa_w_out, ffn_w_in, ffn_w_out, ple_w_proj, ple_w_gate):
    b, s, d = x.shape
    depth = p.shape[0]
    m = b * s
    ret_w_in, ret_w_out, swa_w_in, swa_w_out, ffn_w_in, ffn_w_out, ple_w_proj, ple_w_gate = (
        a.astype(BF16) for a in (ret_w_in, ret_w_out, swa_w_in, swa_w_out, ffn_w_in, ffn_w_out,
                                 ple_w_proj, ple_w_gate))
    p = p.reshape(depth, m, p.shape[-1])
    x = x.reshape(m, d)
    final_gain = norm_final[None, :]
    for i in range(depth):
        j = i // 2
        if i % 2 == 0:
            proj = norm_matmul(x, norm_mix[i][None, :], ret_w_in, j)
            y = retention_core(proj.reshape(b, s, -1), ret_gn[j][:, None, :])
            x = matmul_residual(y.reshape(m, -1), ret_w_out, x, j)
        else:
            proj = norm_matmul(x, norm_mix[i][None, :], swa_w_in, j)
            y = swa_core(proj.reshape(b, s, -1), swa_sinks[j])
            x = matmul_residual(y.reshape(m, -1), swa_w_out, x, j)
        hidden = norm_swiglu(x, norm_ffn[i][None, :], ffn_w_in, i)
        x = matmul_residual(hidden, ffn_w_out, x, i)
        x = ple_update(x, norm_ple[i][None, :], ple_w_gate, p, ple_w_proj, final_gain, i,
                       final=(i == depth - 1))
    return x.reshape(b, s, d)
```

```python
import functools

import jax
import jax.numpy as jnp
from jax import lax
from jax.experimental import pallas as pl
from jax.experimental.pallas import tpu as pltpu

F32 = jnp.float32
BF16 = jnp.bfloat16

NORM_EPS = 1e-6

RET_HEADS = 8
RET_CHUNK = 256
RET_HEADS_PER_STEP = 8
SWA_HEAD_DIM = 64
SWA_KV_HEADS = 4
SWA_WINDOW = 128
SWA_LOOKAHEAD = 3

V7X_VMEM_BYTES = 64 * 1024 * 1024
VMEM_TILE_BUDGET = 54 * 1024 * 1024
VMEM_HEADROOM = 6 * 1024 * 1024
VMEM_LIMIT_BYTES = V7X_VMEM_BYTES - 8 * 1024 * 1024
V7X_MXU_COLS = 256
NORM_ROWS = 128
ROW_TILES = (1024, 512, 256, 128)


def _pick(n, prefs):
    for t in prefs:
        if n % t == 0:
            return t
    return n


def _params(*sem, vmem_bytes=None):
    limit = VMEM_LIMIT_BYTES if vmem_bytes is None else min(vmem_bytes + VMEM_HEADROOM,
                                                            V7X_VMEM_BYTES - 2 * 1024 * 1024)
    return pltpu.CompilerParams(dimension_semantics=sem, vmem_limit_bytes=limit)


def _choose_tiles(m, n, vmem_bytes):
    best = None
    for tm in ROW_TILES:
        if m % tm:
            continue
        for parts in range(1, n // V7X_MXU_COLS + 1):
            tn = n // parts
            if n % parts or tn % V7X_MXU_COLS:
                continue
            need = vmem_bytes(tm, tn)
            if need > VMEM_TILE_BUDGET:
                continue
            key = (tm * tn, tn == n, tm)
            if best is None or key > best[0]:
                best = (key, tm, tn, need)
    if best is None:
        raise ValueError("no matmul tile fits VMEM")
    return best[1:]


def _weight_spec(block, index_map, resident):
    if resident:
        return pl.BlockSpec(block, index_map, pipeline_mode=pl.Buffered(1))
    return pl.BlockSpec(block, index_map)


def _rmsnorm_rows(x, gain):
    inv = lax.rsqrt(jnp.mean(x * x, axis=-1, keepdims=True) + NORM_EPS)
    return x * inv * gain


def _norm_into(x_ref, gain_ref, h_ref):
    tm = x_ref.shape[0]
    rows = min(NORM_ROWS, tm)

    def body(r, carry):
        sl = pl.ds(pl.multiple_of(r * rows, rows), rows)
        h_ref[sl, :] = _rmsnorm_rows(x_ref[sl, :], gain_ref[...]).astype(BF16)
        return carry

    lax.fori_loop(0, tm // rows, body, 0)


def _norm_matmul_kernel(x_ref, gain_ref, w_ref, o_ref, h_ref):
    @pl.when(pl.program_id(1) == 0)
    def _():
        _norm_into(x_ref, gain_ref, h_ref)

    o_ref[...] = jnp.dot(h_ref[...], w_ref[...], preferred_element_type=F32).astype(o_ref.dtype)


def norm_matmul(x, gain, w, layer, *, tm=None, tn=None):
    m, d = x.shape
    n = w.shape[-1]

    def vmem_bytes(tm, tn):
        w_bufs = 1 if tn == n else 2
        return 2 * tm * d * 4 + tm * d * 2 + w_bufs * d * tn * 2 + 2 * tm * tn * 2 + tm * tn * 4

    if tm is None or tn is None:
        tm, tn, need = _choose_tiles(m, n, vmem_bytes)
    else:
        need = vmem_bytes(tm, tn)
    return pl.pallas_call(
        _norm_matmul_kernel,
        grid=(m // tm, n // tn),
        in_specs=[
            pl.BlockSpec((tm, d), lambda i, j: (i, 0)),
            pl.BlockSpec((1, d), lambda i, j: (0, 0)),
            _weight_spec((None, d, tn), lambda i, j: (layer, 0, j), tn == n),
        ],
        out_specs=pl.BlockSpec((tm, tn), lambda i, j: (i, j)),
        out_shape=jax.ShapeDtypeStruct((m, n), BF16),
        scratch_shapes=[pltpu.VMEM((tm, d), BF16)],
        compiler_params=_params("parallel", "arbitrary", vmem_bytes=need),
        name="norm_matmul",
    )(x, gain, w)


def _norm_swiglu_kernel(x_ref, gain_ref, wg_ref, wu_ref, o_ref, h_ref):
    @pl.when(pl.program_id(1) == 0)
    def _():
        _norm_into(x_ref, gain_ref, h_ref)

    h = h_ref[...]
    gate = jnp.dot(h, wg_ref[...], preferred_element_type=F32)
    up = jnp.dot(h, wu_ref[...], preferred_element_type=F32)
    o_ref[...] = (jax.nn.silu(gate) * up).astype(o_ref.dtype)


def norm_swiglu(x, gain, w_in, layer, *, tm=None, tn=None):
    m, d = x.shape
    hd = w_in.shape[-1] // 2

    def vmem_bytes(tm, tn):
        return 2 * tm * d * 4 + tm * d * 2 + 2 * 2 * d * tn * 2 + 2 * tm * tn * 2 + 2 * tm * tn * 4

    if tm is None or tn is None:
        tm, tn, need = _choose_tiles(m, hd, vmem_bytes)
    else:
        need = vmem_bytes(tm, tn)
    nh = hd // tn
    return pl.pallas_call(
        _norm_swiglu_kernel,
        grid=(m // tm, nh),
        in_specs=[
            pl.BlockSpec((tm, d), lambda i, j: (i, 0)),
            pl.BlockSpec((1, d), lambda i, j: (0, 0)),
            pl.BlockSpec((None, d, tn), lambda i, j: (layer, 0, j)),
            pl.BlockSpec((None, d, tn), lambda i, j: (layer, 0, j + nh)),
        ],
        out_specs=pl.BlockSpec((tm, tn), lambda i, j: (i, j)),
        out_shape=jax.ShapeDtypeStruct((m, hd), BF16),
        scratch_shapes=[pltpu.VMEM((tm, d), BF16)],
        compiler_params=_params("parallel", "arbitrary", vmem_bytes=need),
        name="norm_swiglu",
    )(x, gain, w_in, w_in)


def _matmul_residual_kernel(a_ref, w_ref, x_ref, o_ref):
    o_ref[...] = x_ref[...] + jnp.dot(a_ref[...], w_ref[...], preferred_element_type=F32)


def matmul_residual(a, w, x, layer, *, tm=None, tn=None):
    m, k = a.shape
    n = w.shape[-1]

    def vmem_bytes(tm, tn):
        w_bufs = 1 if tn == n else 2
        return 2 * tm * k * 2 + w_bufs * k * tn * 2 + 2 * 2 * tm * tn * 4 + tm * tn * 4

    if tm is None or tn is None:
        tm, tn, need = _choose_tiles(m, n, vmem_bytes)
    else:
        need = vmem_bytes(tm, tn)
    return pl.pallas_call(
        _matmul_residual_kernel,
        grid=(m // tm, n // tn),
        in_specs=[
            pl.BlockSpec((tm, k), lambda i, j: (i, 0)),
            _weight_spec((None, k, tn), lambda i, j: (layer, 0, j), tn == n),
            pl.BlockSpec((tm, tn), lambda i, j: (i, j)),
        ],
        out_specs=pl.BlockSpec((tm, tn), lambda i, j: (i, j)),
        out_shape=jax.ShapeDtypeStruct((m, n), F32),
        compiler_params=_params("parallel", "arbitrary", vmem_bytes=need),
        name="matmul_residual",
    )(a, w, x)


def _ple_kernel(x_ref, gain_ref, wg_ref, p_ref, wp_ref, fgain_ref, o_ref, h_ref, *, final):
    _norm_into(x_ref, gain_ref, h_ref)
    gate = jax.nn.sigmoid(jnp.dot(h_ref[...], wg_ref[...], preferred_element_type=F32))
    pp = jnp.dot(p_ref[...].astype(BF16), wp_ref[...], preferred_element_type=F32)
    xn = x_ref[...] + gate * pp
    if final:
        xn = _rmsnorm_rows(xn, fgain_ref[...])
    o_ref[...] = xn


def ple_update(x, gain, w_gate, p, w_proj, final_gain, layer, *, final, tm=None):
    m, d = x.shape
    pd = p.shape[-1]
    tm = tm or _pick(m, (512, 256, 128))
    return pl.pallas_call(
        functools.partial(_ple_kernel, final=final),
        grid=(m // tm,),
        in_specs=[
            pl.BlockSpec((tm, d), lambda i: (i, 0)),
            pl.BlockSpec((1, d), lambda i: (0, 0)),
            pl.BlockSpec((None, d, d), lambda i: (layer, 0, 0)),
            pl.BlockSpec((None, tm, pd), lambda i: (layer, i, 0)),
            pl.BlockSpec((None, pd, d), lambda i: (layer, 0, 0)),
            pl.BlockSpec((1, d), lambda i: (0, 0)),
        ],
        out_specs=pl.BlockSpec((tm, d), lambda i: (i, 0)),
        out_shape=jax.ShapeDtypeStruct((m, d), F32),
        scratch_shapes=[pltpu.VMEM((tm, d), BF16)],
        compiler_params=_params("parallel"),
        name="ple_update",
    )(x, gain, w_gate, p, w_proj, final_gain)


def _ret_core_kernel(q_ref, k_ref, v_ref, g_ref, dint_ref, qd_ref, kd_ref, cd_ref, gn_ref,
                     o_ref, state_ref, *, heads, chunk):
    dk = q_ref.shape[1] // heads
    dv = v_ref.shape[1] // heads

    @pl.when(pl.program_id(2) == 0)
    def _():
        state_ref[...] = jnp.zeros_like(state_ref)

    def body(ci, carry):
        sl = pl.ds(pl.multiple_of(ci * chunk, chunk), chunk)
        for hh in range(heads):
            ck = slice(hh * dk, (hh + 1) * dk)
            cv = slice(hh * dv, (hh + 1) * dv)
            q = q_ref[sl, ck]
            k = k_ref[sl, ck]
            v = v_ref[sl, cv]
            state = state_ref[hh]
            scores = lax.dot_general(q, k, (((1,), (1,)), ((), ())), preferred_element_type=F32)
            scores = (scores * dint_ref[hh]).astype(BF16)
            q_dec = (q.astype(F32) * qd_ref[hh]).astype(BF16)
            o = (jnp.dot(scores, v, preferred_element_type=F32)
                 + jnp.dot(q_dec, state.astype(BF16), preferred_element_type=F32))
            k_dec = (k.astype(F32) * kd_ref[hh]).astype(BF16)
            state_ref[hh] = state * cd_ref[hh] + lax.dot_general(
                k_dec, v, (((0,), (0,)), ((), ())), preferred_element_type=F32)
            mu = jnp.mean(o, axis=-1, keepdims=True)
            dev = o - mu
            var = jnp.mean(dev * dev, axis=-1, keepdims=True)
            yn = dev * lax.rsqrt(var + NORM_EPS) * gn_ref[hh]
            o_ref[sl, cv] = (jax.nn.silu(g_ref[sl, cv].astype(F32)) * yn).astype(o_ref.dtype)
        return carry

    lax.fori_loop(0, q_ref.shape[0] // chunk, body, 0)


def _retention_decays(c, dk, dv):
    h = RET_HEADS
    log_g = jnp.log1p(-jnp.exp2(-5.0 - jnp.arange(h, dtype=F32)))
    pos = jnp.arange(c, dtype=F32)
    diff = pos[:, None] - pos[None, :]
    causal = diff >= 0
    decay_intra = jnp.where(causal, jnp.exp(log_g[:, None, None] * jnp.where(causal, diff, 0.0)), 0.0)
    q_decay = jnp.exp(log_g[:, None] * (pos + 1.0))[:, :, None]
    k_decay = jnp.exp(log_g[:, None] * (c - 1.0 - pos))[:, :, None]
    chunk_decay = jnp.exp(log_g * c)[:, None, None]
    k_scale = dk ** -0.5
    return (decay_intra * k_scale,
            jnp.broadcast_to(q_decay, (h, c, dk)),
            jnp.broadcast_to(k_decay * k_scale, (h, c, dk)),
            jnp.broadcast_to(chunk_decay, (h, 1, dv)))


def retention_core(proj, gn_gain, *, ts=None, heads=RET_HEADS_PER_STEP, chunk=RET_CHUNK):
    b, s, width = proj.shape
    h = RET_HEADS
    dk = width // (6 * h)
    dv = 2 * dk
    ts = ts or _pick(s, (512, 256, 128))
    chunk = min(chunk, ts)
    hg = h // heads
    dint, qd, kd, cd = _retention_decays(chunk, dk, dv)
    head_table = lambda last: pl.BlockSpec((heads,) + last, lambda bi, hi, ti: (hi, 0, 0))
    return pl.pallas_call(
        functools.partial(_ret_core_kernel, heads=heads, chunk=chunk),
        grid=(b, hg, s // ts),
        in_specs=[
            pl.BlockSpec((None, ts, heads * dk), lambda bi, hi, ti: (bi, ti, hi)),
            pl.BlockSpec((None, ts, heads * dk), lambda bi, hi, ti: (bi, ti, hg + hi)),
            pl.BlockSpec((None, ts, heads * dv), lambda bi, hi, ti: (bi, ti, hg + hi)),
            pl.BlockSpec((None, ts, heads * dv), lambda bi, hi, ti: (bi, ti, 2 * hg + hi)),
            head_table((chunk, chunk)),
            head_table((chunk, dk)),
            head_table((chunk, dk)),
            head_table((1, dv)),
            head_table((1, dv)),
        ],
        out_specs=pl.BlockSpec((None, ts, heads * dv), lambda bi, hi, ti: (bi, ti, hi)),
        out_shape=jax.ShapeDtypeStruct((b, s, h * dv), BF16),
        scratch_shapes=[pltpu.VMEM((heads, dk, dv), F32)],
        compiler_params=_params("parallel", "parallel", "arbitrary"),
        name="retention_core",
    )(proj, proj, proj, proj, dint, qd, kd, cd, gn_gain)


def _swa_core_kernel(sinks_ref, q_ref, kp_ref, kc_ref, vp_ref, vc_ref, bias_ref, o_ref,
                     kb_ref, vb_ref, *, n_heads):
    w, hd = SWA_WINDOW, SWA_HEAD_DIM
    group = n_heads // SWA_KV_HEADS
    ts = q_ref.shape[0]
    q_scale = hd ** -0.5
    kb_ref[0:w, :] = kp_ref[...] * q_scale
    kb_ref[w:, :] = kc_ref[...] * q_scale
    vb_ref[0:w, :] = vp_ref[...]
    vb_ref[w:, :] = vc_ref[...]

    row = lax.broadcasted_iota(jnp.int32, (w, w), 0)
    lane = lax.broadcasted_iota(jnp.int32, (w, w), 1)
    upper = lane > row
    left = lane < hd
    zero = jnp.zeros((w, w), BF16)
    upper2 = jnp.concatenate([upper, upper], axis=1)
    zero2 = jnp.zeros((w, 2 * w), BF16)
    ones_diag = jnp.concatenate([jnp.where(left, 1.0, 0.0), jnp.where(left, 0.0, 1.0)],
                                axis=0).astype(BF16)
    first_tile = pl.program_id(1) == 0
    nt = (((1,), (1,)), ((), ()))

    def split(slab):
        swapped = pltpu.roll(slab, hd, axis=1)
        return ((jnp.where(left, slab, zero), jnp.where(left, zero, swapped)),
                (jnp.where(left, swapped, zero), jnp.where(left, zero, slab)))

    def body(n, carry):
        row0 = pl.multiple_of(n * w, w)
        fidx = jnp.logical_and(first_tile, n == 0).astype(jnp.int32)
        rows = pl.ds(row0, w)
        prev_rows = pl.ds(row0, w)
        cur_rows = pl.ds(row0 + w, w)
        kprev, kcur, vprev, vcur = [], [], [], []
        for slab_i in range(SWA_KV_HEADS // 2):
            cols = slice(slab_i * 2 * hd, (slab_i + 1) * 2 * hd)
            kprev.extend(split(kb_ref[prev_rows, cols]))
            kcur.extend(split(kb_ref[cur_rows, cols]))
            vprev.extend(jnp.concatenate([jnp.concatenate(t, axis=0), ones_diag], axis=1)
                         for t in split(vb_ref[prev_rows, cols]))
            vcur.extend(jnp.concatenate([jnp.concatenate(t, axis=0), ones_diag], axis=1)
                        for t in split(vb_ref[cur_rows, cols]))

        def scores(pair):
            kv = (2 * pair) // group
            qp = q_ref[rows, 2 * pair * hd:(2 * pair + 2) * hd]
            out = []
            for side in range(2):
                sp = lax.dot_general(qp, kprev[kv][side], nt, preferred_element_type=F32)
                sc = lax.dot_general(qp, kcur[kv][side], nt, preferred_element_type=F32)
                out.append(jnp.where(upper, sp, sc) + bias_ref[fidx, 2 * pair + side])
            return out

        def softmax_numerators(pair, s_pair):
            out = []
            for side in range(2):
                sink = sinks_ref[2 * pair + side]
                m = jnp.maximum(jnp.max(s_pair[side], axis=-1, keepdims=True), sink)
                out.append((jnp.exp(s_pair[side] - m).astype(BF16), jnp.exp(sink - m)))
            return out

        def weighted_values(pair, e_pair):
            kv = (2 * pair) // group
            e = jnp.concatenate([e_pair[0][0], e_pair[1][0]], axis=1)
            both = (jnp.dot(jnp.where(upper2, e, zero2), vprev[kv], preferred_element_type=F32)
                    + jnp.dot(jnp.where(upper2, zero2, e), vcur[kv], preferred_element_type=F32))
            acc = both[:, :w]
            den = both[:, w:] + jnp.where(left, e_pair[0][1], e_pair[1][1])
            o_ref[rows, 2 * pair * hd:(2 * pair + 2) * hd] = (acc * (1.0 / den)).astype(o_ref.dtype)

        n_pairs = n_heads // 2
        s_tiles, e_tiles = {}, {}
        for t in range(n_pairs + 2 * SWA_LOOKAHEAD):
            if t < n_pairs:
                s_tiles[t] = scores(t)
            t1 = t - SWA_LOOKAHEAD
            if 0 <= t1 < n_pairs:
                e_tiles[t1] = softmax_numerators(t1, s_tiles.pop(t1))
            t2 = t1 - SWA_LOOKAHEAD
            if 0 <= t2 < n_pairs:
                weighted_values(t2, e_tiles.pop(t2))
        return carry

    lax.fori_loop(0, ts // w, body, 0)


def _swa_bias(n_heads):
    w = SWA_WINDOW
    slopes = jnp.exp2(-8.0 * (jnp.arange(n_heads, dtype=F32) + 1.0) / n_heads)
    qi = jnp.arange(w)[:, None]
    kc = jnp.arange(w)[None, :]
    upper = kc > qi
    rel = jnp.where(upper, qi - kc + w, qi - kc).astype(F32)
    bias = -slopes[:, None, None] * rel
    return jnp.stack([bias, jnp.where(upper, -jnp.inf, bias)])


def swa_core(proj, sinks, *, ts=None):
    b, s, width = proj.shape
    w = SWA_WINDOW
    kvw = SWA_KV_HEADS * SWA_HEAD_DIM
    qw = width - 2 * kvw
    n_heads = qw // SWA_HEAD_DIM
    ts = ts or _pick(s, (512, 256, 128))
    bpt = ts // w
    prev = lambda col: (lambda bi, ti: (bi, jnp.maximum(ti * bpt - 1, 0), col))
    cur = lambda col: (lambda bi, ti: (bi, ti, col))
    return pl.pallas_call(
        functools.partial(_swa_core_kernel, n_heads=n_heads),
        grid=(b, s // ts),
        in_specs=[
            pl.BlockSpec(memory_space=pltpu.SMEM),
            pl.BlockSpec((None, ts, qw), lambda bi, ti: (bi, ti, 0)),
            pl.BlockSpec((None, w, kvw), prev(qw // kvw)),
            pl.BlockSpec((None, ts, kvw), cur(qw // kvw)),
            pl.BlockSpec((None, w, kvw), prev(qw // kvw + 1)),
            pl.BlockSpec((None, ts, kvw), cur(qw // kvw + 1)),
            pl.BlockSpec((2, n_heads, w, w), lambda bi, ti: (0, 0, 0, 0)),
        ],
        out_specs=pl.BlockSpec((None, ts, qw), lambda bi, ti: (bi, ti, 0)),
        out_shape=jax.ShapeDtypeStruct((b, s, qw), BF16),
        scratch_shapes=[pltpu.VMEM((ts + w, kvw), BF16), pltpu.VMEM((ts + w, kvw), BF16)],
        compiler_params=_params("parallel", "arbitrary"),
        name="swa_core",
    )(sinks, proj, proj, proj, proj, proj, _swa_bias(n_heads))


def kernel(x, p, norm_mix, norm_ffn, norm_ple, norm_final, ret_w_in, ret_gn, ret_w_out,
           swa_w_in, swa_sinks, swa_w_out, ffn_w_in, ffn_w_out, ple_w_proj, ple_w_gate):
    b, s, d = x.shape
    depth = p.shape[0]
    m = b * s
    ret_w_in, ret_w_out, swa_w_in, swa_w_out, ffn_w_in, ffn_w_out, ple_w_proj, ple_w_gate = (
        a.astype(BF16) for a in (ret_w_in, ret_w_out, swa_w_in, swa_w_out, ffn_w_in, ffn_w_out,
                                 ple_w_proj, ple_w_gate))
    p = p.reshape(depth, m, p.shape[-1])
    x = x.reshape(m, d)
    final_gain = norm_final[None, :]
    for i in range(depth):
        j = i // 2
        if i % 2 == 0:
            proj = norm_matmul(x, norm_mix[i][None, :], ret_w_in, j)
            y = retention_core(proj.reshape(b, s, -1), ret_gn[j][:, None, :])
            x = matmul_residual(y.reshape(m, -1), ret_w_out, x, j)
        else:
            proj = norm_matmul(x, norm_mix[i][None, :], swa_w_in, j)
            y = swa_core(proj.reshape(b, s, -1), swa_sinks[j])
            x = matmul_residual(y.reshape(m, -1), swa_w_out, x, j)
        hidden = norm_swiglu(x, norm_ffn[i][None, :], ffn_w_in, i)
        x = matmul_residual(hidden, ffn_w_out, x, i)
        x = ple_update(x, norm_ple[i][None, :], ple_w_gate, p, ple_w_proj, final_gain, i,
                       final=(i == depth - 1))
    return x.reshape(b, s, d)
```

```python
import functools

import jax
import jax.numpy as jnp
from jax import lax
from jax.experimental import pallas as pl
from jax.experimental.pallas import tpu as pltpu

F32 = jnp.float32
BF16 = jnp.bfloat16

NORM_EPS = 1e-6

RET_HEADS = 8
RET_CHUNK = 256
RET_HEADS_PER_STEP = 8
SWA_HEAD_DIM = 64
SWA_KV_HEADS = 4
SWA_WINDOW = 128
SWA_LOOKAHEAD = 3

V7X_VMEM_BYTES = 64 * 1024 * 1024
VMEM_TILE_BUDGET = 50 * 1024 * 1024
VMEM_HEADROOM = 8 * 1024 * 1024
VMEM_LIMIT_BYTES = V7X_VMEM_BYTES - 8 * 1024 * 1024
V7X_MXU_COLS = 256
NORM_ROWS = 128
NORM_ALIGN = 32
PLE_COLS = 512
ROW_TILES = (1024, 512, 256, 128)


def _pick(n, prefs):
    for t in prefs:
        if n % t == 0:
            return t
    return n


def _params(*sem, vmem_bytes=None):
    limit = VMEM_LIMIT_BYTES if vmem_bytes is None else min(vmem_bytes + VMEM_HEADROOM,
                                                            V7X_VMEM_BYTES - 2 * 1024 * 1024)
    return pltpu.CompilerParams(dimension_semantics=sem, vmem_limit_bytes=limit)


def _choose_tiles(m, n, vmem_bytes):
    best = None
    for tm in ROW_TILES:
        if m % tm:
            continue
        for parts in range(1, n // V7X_MXU_COLS + 1):
            tn = n // parts
            if n % parts or tn % V7X_MXU_COLS:
                continue
            need = vmem_bytes(tm, tn)
            if need > VMEM_TILE_BUDGET:
                continue
            key = (tm * tn, tn == n, tm)
            if best is None or key > best[0]:
                best = (key, tm, tn, need)
    if best is None:
        raise ValueError("no matmul tile fits VMEM")
    return best[1:]


def _weight_spec(block, index_map, resident):
    if resident:
        return pl.BlockSpec(block, index_map, pipeline_mode=pl.Buffered(1))
    return pl.BlockSpec(block, index_map)


def _rmsnorm_rows(x, gain):
    inv = lax.rsqrt(jnp.mean(x * x, axis=-1, keepdims=True) + NORM_EPS)
    return x * inv * gain


def _norm_share(x_ref, gain_ref, h_ref, step, n_steps):
    tm = x_ref.shape[0]
    rows = -(-tm // n_steps)
    rows = min(tm, -(-rows // NORM_ALIGN) * NORM_ALIGN)
    start = pl.multiple_of(jnp.minimum(step * rows, tm - rows), NORM_ALIGN)
    for r in range(0, rows, NORM_ROWS):
        sl = pl.ds(start + r, min(NORM_ROWS, rows - r))
        h_ref[sl, :] = _rmsnorm_rows(x_ref[sl, :], gain_ref[...]).astype(BF16)


def _lagged_tiles(norm_into, matmul_from, bufs, n_tiles):
    i = pl.program_id(0)

    @pl.when(i == 0)
    def _():
        norm_into(bufs[0])

    for parity in (0, 1):
        @pl.when(jnp.logical_and(jnp.logical_and(i > 0, i < n_tiles), i % 2 == parity))
        def _(parity=parity):
            matmul_from(bufs[1 - parity])
            norm_into(bufs[parity])

    @pl.when(i == n_tiles)
    def _():
        matmul_from(bufs[(n_tiles - 1) % 2])


def _lagged_index_maps(n_tiles, layer):
    x_map = lambda i, j: (jnp.minimum(i, n_tiles - 1), 0)
    col = lambda i, j: jnp.where(i == 0, 0, j)
    w_map = lambda shift: (lambda i, j: (layer, 0, col(i, j) + shift))
    o_map = lambda i, j: (jnp.maximum(i - 1, 0), col(i, j))
    return x_map, w_map, o_map


def _norm_matmul_kernel(x_ref, gain_ref, w_ref, o_ref, ha_ref, hb_ref, *, n_tiles, n_cols):
    col = pl.program_id(1)

    def norm_into(h_ref):
        _norm_share(x_ref, gain_ref, h_ref, col, n_cols)

    def matmul_from(h_ref):
        o_ref[...] = jnp.dot(h_ref[...], w_ref[...], preferred_element_type=F32).astype(o_ref.dtype)

    _lagged_tiles(norm_into, matmul_from, (ha_ref, hb_ref), n_tiles)


def norm_matmul(x, gain, w, layer, *, tm=None, tn=None):
    m, d = x.shape
    n = w.shape[-1]

    def vmem_bytes(tm, tn):
        w_bufs = 1 if tn == n else 2
        return 2 * tm * d * 4 + 2 * tm * d * 2 + w_bufs * d * tn * 2 + 2 * tm * tn * 2

    if tm is None or tn is None:
        tm, tn, need = _choose_tiles(m, n, vmem_bytes)
    else:
        need = vmem_bytes(tm, tn)
    n_tiles, n_cols = m // tm, n // tn
    x_map, w_map, o_map = _lagged_index_maps(n_tiles, layer)
    return pl.pallas_call(
        functools.partial(_norm_matmul_kernel, n_tiles=n_tiles, n_cols=n_cols),
        grid=(n_tiles + 1, n_cols),
        in_specs=[
            pl.BlockSpec((tm, d), x_map),
            pl.BlockSpec((1, d), lambda i, j: (0, 0)),
            _weight_spec((None, d, tn), w_map(0), tn == n),
        ],
        out_specs=pl.BlockSpec((tm, tn), o_map),
        out_shape=jax.ShapeDtypeStruct((m, n), BF16),
        scratch_shapes=[pltpu.VMEM((tm, d), BF16), pltpu.VMEM((tm, d), BF16)],
        compiler_params=_params("arbitrary", "arbitrary", vmem_bytes=need),
        name="norm_matmul",
    )(x, gain, w)


def _norm_swiglu_kernel(x_ref, gain_ref, wg_ref, wu_ref, o_ref, ha_ref, hb_ref, *, n_tiles, n_cols):
    col = pl.program_id(1)

    def norm_into(h_ref):
        _norm_share(x_ref, gain_ref, h_ref, col, n_cols)

    def matmul_from(h_ref):
        h = h_ref[...]
        gate = jnp.dot(h, wg_ref[...], preferred_element_type=F32)
        up = jnp.dot(h, wu_ref[...], preferred_element_type=F32)
        o_ref[...] = (jax.nn.silu(gate) * up).astype(o_ref.dtype)

    _lagged_tiles(norm_into, matmul_from, (ha_ref, hb_ref), n_tiles)


def norm_swiglu(x, gain, w_in, layer, *, tm=None, tn=None):
    m, d = x.shape
    hd = w_in.shape[-1] // 2

    def vmem_bytes(tm, tn):
        return 2 * tm * d * 4 + 2 * tm * d * 2 + 2 * 2 * d * tn * 2 + 2 * tm * tn * 2

    if tm is None or tn is None:
        tm, tn, need = _choose_tiles(m, hd, vmem_bytes)
    else:
        need = vmem_bytes(tm, tn)
    n_tiles, n_cols = m // tm, hd // tn
    x_map, w_map, o_map = _lagged_index_maps(n_tiles, layer)
    return pl.pallas_call(
        functools.partial(_norm_swiglu_kernel, n_tiles=n_tiles, n_cols=n_cols),
        grid=(n_tiles + 1, n_cols),
        in_specs=[
            pl.BlockSpec((tm, d), x_map),
            pl.BlockSpec((1, d), lambda i, j: (0, 0)),
            pl.BlockSpec((None, d, tn), w_map(0)),
            pl.BlockSpec((None, d, tn), w_map(n_cols)),
        ],
        out_specs=pl.BlockSpec((tm, tn), o_map),
        out_shape=jax.ShapeDtypeStruct((m, hd), BF16),
        scratch_shapes=[pltpu.VMEM((tm, d), BF16), pltpu.VMEM((tm, d), BF16)],
        compiler_params=_params("arbitrary", "arbitrary", vmem_bytes=need),
        name="norm_swiglu",
    )(x, gain, w_in, w_in)


def _matmul_residual_kernel(a_ref, w_ref, x_ref, o_ref):
    o_ref[...] = x_ref[...] + jnp.dot(a_ref[...], w_ref[...], preferred_element_type=F32)


def matmul_residual(a, w, x, layer, *, tm=None, tn=None):
    m, k = a.shape
    n = w.shape[-1]

    def vmem_bytes(tm, tn):
        w_bufs = 1 if tn == n else 2
        return 2 * tm * k * 2 + w_bufs * k * tn * 2 + 2 * 2 * tm * tn * 4

    if tm is None or tn is None:
        tm, tn, need = _choose_tiles(m, n, vmem_bytes)
    else:
        need = vmem_bytes(tm, tn)
    return pl.pallas_call(
        _matmul_residual_kernel,
        grid=(m // tm, n // tn),
        in_specs=[
            pl.BlockSpec((tm, k), lambda i, j: (i, 0)),
            _weight_spec((None, k, tn), lambda i, j: (layer, 0, j), tn == n),
            pl.BlockSpec((tm, tn), lambda i, j: (i, j)),
        ],
        out_specs=pl.BlockSpec((tm, tn), lambda i, j: (i, j)),
        out_shape=jax.ShapeDtypeStruct((m, n), F32),
        compiler_params=_params("parallel", "arbitrary", vmem_bytes=need),
        name="matmul_residual",
    )(a, w, x)


def _ple_kernel(xn_ref, xr_ref, gain_ref, wg_ref, p_ref, wp_ref, fgain_ref, o_ref, ha_ref, hb_ref,
                *, n_tiles, final):
    tm, d = o_ref.shape

    def norm_into(h_ref):
        _norm_share(xn_ref, gain_ref, h_ref, 0, 1)

    def matmul_from(h_ref):
        h = h_ref[...]
        pb = p_ref[...].astype(BF16)
        for c in range(0, d, PLE_COLS):
            cols = slice(c, c + PLE_COLS)
            gate = jax.nn.sigmoid(jnp.dot(h, wg_ref[:, cols], preferred_element_type=F32))
            pp = jnp.dot(pb, wp_ref[:, cols], preferred_element_type=F32)
            o_ref[:, cols] = xr_ref[:, cols] + gate * pp
        if final:
            for r in range(0, tm, NORM_ROWS):
                rows = slice(r, r + NORM_ROWS)
                o_ref[rows, :] = _rmsnorm_rows(o_ref[rows, :], fgain_ref[...])

    _lagged_tiles(norm_into, matmul_from, (ha_ref, hb_ref), n_tiles)


def ple_update(x, gain, w_gate, p, w_proj, final_gain, layer, *, final, tm=None):
    m, d = x.shape
    pd = p.shape[-1]
    tm = tm or _pick(m, (512, 256, 128))
    n_tiles = m // tm
    need = 3 * 2 * tm * d * 4 + d * d * 2 + pd * d * 2 + 2 * tm * pd * 4 + 2 * tm * d * 2
    ahead = lambda i: (jnp.minimum(i, n_tiles - 1), 0)
    behind = lambda i: (jnp.maximum(i - 1, 0), 0)
    resident = lambda i: (layer, 0, 0)
    return pl.pallas_call(
        functools.partial(_ple_kernel, n_tiles=n_tiles, final=final),
        grid=(n_tiles + 1,),
        in_specs=[
            pl.BlockSpec((tm, d), ahead),
            pl.BlockSpec((tm, d), behind),
            pl.BlockSpec((1, d), lambda i: (0, 0)),
            _weight_spec((None, d, d), resident, True),
            pl.BlockSpec((None, tm, pd), lambda i: (layer, jnp.maximum(i - 1, 0), 0)),
            _weight_spec((None, pd, d), resident, True),
            pl.BlockSpec((1, d), lambda i: (0, 0)),
        ],
        out_specs=pl.BlockSpec((tm, d), behind),
        out_shape=jax.ShapeDtypeStruct((m, d), F32),
        scratch_shapes=[pltpu.VMEM((tm, d), BF16), pltpu.VMEM((tm, d), BF16)],
        compiler_params=_params("arbitrary", vmem_bytes=need),
        name="ple_update",
    )(x, x, gain, w_gate, p, w_proj, final_gain)


def _ret_core_kernel(q_ref, k_ref, v_ref, g_ref, dint_ref, qd_ref, kd_ref, cd_ref, gn_ref,
                     o_ref, state_ref, *, heads, chunk):
    dk = q_ref.shape[1] // heads
    dv = v_ref.shape[1] // heads

    @pl.when(pl.program_id(2) == 0)
    def _():
        state_ref[...] = jnp.zeros_like(state_ref)

    def body(ci, carry):
        sl = pl.ds(pl.multiple_of(ci * chunk, chunk), chunk)
        for hh in range(heads):
            ck = slice(hh * dk, (hh + 1) * dk)
            cv = slice(hh * dv, (hh + 1) * dv)
            q = q_ref[sl, ck]
            k = k_ref[sl, ck]
            v = v_ref[sl, cv]
            state = state_ref[hh]
            scores = lax.dot_general(q, k, (((1,), (1,)), ((), ())), preferred_element_type=F32)
            scores = (scores * dint_ref[hh]).astype(BF16)
            q_dec = (q.astype(F32) * qd_ref[hh]).astype(BF16)
            o = (jnp.dot(scores, v, preferred_element_type=F32)
                 + jnp.dot(q_dec, state.astype(BF16), preferred_element_type=F32))
            k_dec = (k.astype(F32) * kd_ref[hh]).astype(BF16)
            state_ref[hh] = state * cd_ref[hh] + lax.dot_general(
                k_dec, v, (((0,), (0,)), ((), ())), preferred_element_type=F32)
            mu = jnp.mean(o, axis=-1, keepdims=True)
            dev = o - mu
            var = jnp.mean(dev * dev, axis=-1, keepdims=True)
            yn = dev * lax.rsqrt(var + NORM_EPS) * gn_ref[hh]
            o_ref[sl, cv] = (jax.nn.silu(g_ref[sl, cv].astype(F32)) * yn).astype(o_ref.dtype)
        return carry

    lax.fori_loop(0, q_ref.shape[0] // chunk, body, 0)


def _retention_decays(c, dk, dv):
    h = RET_HEADS
    log_g = jnp.log1p(-jnp.exp2(-5.0 - jnp.arange(h, dtype=F32)))
    pos = jnp.arange(c, dtype=F32)
    diff = pos[:, None] - pos[None, :]
    causal = diff >= 0
    decay_intra = jnp.where(causal, jnp.exp(log_g[:, None, None] * jnp.where(causal, diff, 0.0)), 0.0)
    q_decay = jnp.exp(log_g[:, None] * (pos + 1.0))[:, :, None]
    k_decay = jnp.exp(log_g[:, None] * (c - 1.0 - pos))[:, :, None]
    chunk_decay = jnp.exp(log_g * c)[:, None, None]
    k_scale = dk ** -0.5
    return (decay_intra * k_scale,
            jnp.broadcast_to(q_decay, (h, c, dk)),
            jnp.broadcast_to(k_decay * k_scale, (h, c, dk)),
            jnp.broadcast_to(chunk_decay, (h, 1, dv)))


def retention_core(proj, gn_gain, *, ts=None, heads=RET_HEADS_PER_STEP, chunk=RET_CHUNK):
    b, s, width = proj.shape
    h = RET_HEADS
    dk = width // (6 * h)
    dv = 2 * dk
    ts = ts or _pick(s, (512, 256, 128))
    chunk = min(chunk, ts)
    hg = h // heads
    dint, qd, kd, cd = _retention_decays(chunk, dk, dv)
    head_table = lambda last: pl.BlockSpec((heads,) + last, lambda bi, hi, ti: (hi, 0, 0))
    return pl.pallas_call(
        functools.partial(_ret_core_kernel, heads=heads, chunk=chunk),
        grid=(b, hg, s // ts),
        in_specs=[
            pl.BlockSpec((None, ts, heads * dk), lambda bi, hi, ti: (bi, ti, hi)),
            pl.BlockSpec((None, ts, heads * dk), lambda bi, hi, ti: (bi, ti, hg + hi)),
            pl.BlockSpec((None, ts, heads * dv), lambda bi, hi, ti: (bi, ti, hg + hi)),
            pl.BlockSpec((None, ts, heads * dv), lambda bi, hi, ti: (bi, ti, 2 * hg + hi)),
            head_table((chunk, chunk)),
            head_table((chunk, dk)),
            head_table((chunk, dk)),
            head_table((1, dv)),
            head_table((1, dv)),
        ],
        out_specs=pl.BlockSpec((None, ts, heads * dv), lambda bi, hi, ti: (bi, ti, hi)),
        out_shape=jax.ShapeDtypeStruct((b, s, h * dv), BF16),
        scratch_shapes=[pltpu.VMEM((heads, dk, dv), F32)],
        compiler_params=_params("parallel", "parallel", "arbitrary"),
        name="retention_core",
    )(proj, proj, proj, proj, dint, qd, kd, cd, gn_gain)


def _swa_core_kernel(sinks_ref, q_ref, kp_ref, kc_ref, vp_ref, vc_ref, bias_ref, o_ref,
                     kb_ref, vb_ref, *, n_heads):
    w, hd = SWA_WINDOW, SWA_HEAD_DIM
    group = n_heads // SWA_KV_HEADS
    ts = q_ref.shape[0]
    q_scale = hd ** -0.5
    kb_ref[0:w, :] = kp_ref[...] * q_scale
    kb_ref[w:, :] = kc_ref[...] * q_scale
    vb_ref[0:w, :] = vp_ref[...]
    vb_ref[w:, :] = vc_ref[...]

    row = lax.broadcasted_iota(jnp.int32, (w, w), 0)
    lane = lax.broadcasted_iota(jnp.int32, (w, w), 1)
    upper = lane > row
    left = lane < hd
    zero = jnp.zeros((w, w), BF16)
    upper2 = jnp.concatenate([upper, upper], axis=1)
    zero2 = jnp.zeros((w, 2 * w), BF16)
    ones_diag = jnp.concatenate([jnp.where(left, 1.0, 0.0), jnp.where(left, 0.0, 1.0)],
                                axis=0).astype(BF16)
    first_tile = pl.program_id(1) == 0
    nt = (((1,), (1,)), ((), ()))

    def split(slab):
        swapped = pltpu.roll(slab, hd, axis=1)
        return ((jnp.where(left, slab, zero), jnp.where(left, zero, swapped)),
                (jnp.where(left, swapped, zero), jnp.where(left, zero, slab)))

    def body(n, carry):
        row0 = pl.multiple_of(n * w, w)
        fidx = jnp.logical_and(first_tile, n == 0).astype(jnp.int32)
        rows = pl.ds(row0, w)
        prev_rows = pl.ds(row0, w)
        cur_rows = pl.ds(row0 + w, w)
        kprev, kcur, vprev, vcur = [], [], [], []
        for slab_i in range(SWA_KV_HEADS // 2):
            cols = slice(slab_i * 2 * hd, (slab_i + 1) * 2 * hd)
            kprev.extend(split(kb_ref[prev_rows, cols]))
            kcur.extend(split(kb_ref[cur_rows, cols]))
            vprev.extend(jnp.concatenate([jnp.concatenate(t, axis=0), ones_diag], axis=1)
                         for t in split(vb_ref[prev_rows, cols]))
            vcur.extend(jnp.concatenate([jnp.concatenate(t, axis=0), ones_diag], axis=1)
                        for t in split(vb_ref[cur_rows, cols]))

        def scores(pair):
            kv = (2 * pair) // group
            qp = q_ref[rows, 2 * pair * hd:(2 * pair + 2) * hd]
            out = []
            for side in range(2):
                sp = lax.dot_general(qp, kprev[kv][side], nt, preferred_element_type=F32)
                sc = lax.dot_general(qp, kcur[kv][side], nt, preferred_element_type=F32)
                out.append(jnp.where(upper, sp, sc) + bias_ref[fidx, 2 * pair + side])
            return out

        def softmax_numerators(pair, s_pair):
            out = []
            for side in range(2):
                sink = sinks_ref[2 * pair + side]
                m = jnp.maximum(jnp.max(s_pair[side], axis=-1, keepdims=True), sink)
                out.append((jnp.exp(s_pair[side] - m).astype(BF16), jnp.exp(sink - m)))
            return out

        def weighted_values(pair, e_pair):
            kv = (2 * pair) // group
            e = jnp.concatenate([e_pair[0][0], e_pair[1][0]], axis=1)
            both = (jnp.dot(jnp.where(upper2, e, zero2), vprev[kv], preferred_element_type=F32)
                    + jnp.dot(jnp.where(upper2, zero2, e), vcur[kv], preferred_element_type=F32))
            acc = both[:, :w]
            den = both[:, w:] + jnp.where(left, e_pair[0][1], e_pair[1][1])
            o_ref[rows, 2 * pair * hd:(2 * pair + 2) * hd] = (acc * (1.0 / den)).astype(o_ref.dtype)

        n_pairs = n_heads // 2
        s_tiles, e_tiles = {}, {}
        for t in range(n_pairs + 2 * SWA_LOOKAHEAD):
            if t < n_pairs:
                s_tiles[t] = scores(t)
            t1 = t - SWA_LOOKAHEAD
            if 0 <= t1 < n_pairs:
                e_tiles[t1] = softmax_numerators(t1, s_tiles.pop(t1))
            t2 = t1 - SWA_LOOKAHEAD
            if 0 <= t2 < n_pairs:
                weighted_values(t2, e_tiles.pop(t2))
        return carry

    lax.fori_loop(0, ts // w, body, 0)


def _swa_bias(n_heads):
    w = SWA_WINDOW
    slopes = jnp.exp2(-8.0 * (jnp.arange(n_heads, dtype=F32) + 1.0) / n_heads)
    qi = jnp.arange(w)[:, None]
    kc = jnp.arange(w)[None, :]
    upper = kc > qi
    rel = jnp.where(upper, qi - kc + w, qi - kc).astype(F32)
    bias = -slopes[:, None, None] * rel
    return jnp.stack([bias, jnp.where(upper, -jnp.inf, bias)])


def swa_core(proj, sinks, *, ts=None):
    b, s, width = proj.shape
    w = SWA_WINDOW
    kvw = SWA_KV_HEADS * SWA_HEAD_DIM
    qw = width - 2 * kvw
    n_heads = qw // SWA_HEAD_DIM
    ts = ts or _pick(s, (512, 256, 128))
    bpt = ts // w
    prev = lambda col: (lambda bi, ti: (bi, jnp.maximum(ti * bpt - 1, 0), col))
    cur = lambda col: (lambda bi, ti: (bi, ti, col))
    return pl.pallas_call(
        functools.partial(_swa_core_kernel, n_heads=n_heads),
        grid=(b, s // ts),
        in_specs=[
            pl.BlockSpec(memory_space=pltpu.SMEM),
            pl.BlockSpec((None, ts, qw), lambda bi, ti: (bi, ti, 0)),
            pl.BlockSpec((None, w, kvw), prev(qw // kvw)),
            pl.BlockSpec((None, ts, kvw), cur(qw // kvw)),
            pl.BlockSpec((None, w, kvw), prev(qw // kvw + 1)),
            pl.BlockSpec((None, ts, kvw), cur(qw // kvw + 1)),
            pl.BlockSpec((2, n_heads, w, w), lambda bi, ti: (0, 0, 0, 0)),
        ],
        out_specs=pl.BlockSpec((None, ts, qw), lambda bi, ti: (bi, ti, 0)),
        out_shape=jax.ShapeDtypeStruct((b, s, qw), BF16),
        scratch_shapes=[pltpu.VMEM((ts + w, kvw), BF16), pltpu.VMEM((ts + w, kvw), BF16)],
        compiler_params=_params("parallel", "arbitrary"),
        name="swa_core",
    )(sinks, proj, proj, proj, proj, proj, _swa_bias(n_heads))


def kernel(x, p, norm_mix, norm_ffn, norm_ple, norm_final, ret_w_in, ret_gn, ret_w_out,
           swa_w_in, swa_sinks, swa_w_out, ffn_w_in, ffn_w_out, ple_w_proj, ple_w_gate):
    b, s, d = x.shape
    depth = p.shape[0]
    m = b * s
    ret_w_in, ret_w_out, swa_w_in, swa_w_out, ffn_w_in, ffn_w_out, ple_w_proj, ple_w_gate = (
        a.astype(BF16) for a in (ret_w_in, ret_w_out, swa_w_in, swa_w_out, ffn_w_in, ffn_w_out,
                                 ple_w_proj, ple_w_gate))
    p = p.reshape(depth, m, p.shape[-1])
    x = x.reshape(m, d)
    final_gain = norm_final[None, :]
    for i in range(depth):
        j = i // 2
        if i % 2 == 0:
            proj = norm_matmul(x, norm_mix[i][None, :], ret_w_in, j)
            y = retention_core(proj.reshape(b, s, -1), ret_gn[j][:, None, :])
            x = matmul_residual(y.reshape(m, -1), ret_w_out, x, j)
        else:
            proj = norm_matmul(x, norm_mix[i][None, :], swa_w_in, j)
            y = swa_core(proj.reshape(b, s, -1), swa_sinks[j])
            x = matmul_residual(y.reshape(m, -1), swa_w_out, x, j)
        hidden = norm_swiglu(x, norm_ffn[i][None, :], ffn_w_in, i)
        x = matmul_residual(hidden, ffn_w_out, x, i)
        x = ple_update(x, norm_ple[i][None, :], ple_w_gate, p, ple_w_proj, final_gain, i,
                       final=(i == depth - 1))
    return x.reshape(b, s, d)
```

```python
import functools

import jax
import jax.numpy as jnp
from jax import lax
from jax.experimental import pallas as pl
from jax.experimental.pallas import tpu as pltpu

F32 = jnp.float32
BF16 = jnp.bfloat16

NORM_EPS = 1e-6

RET_HEADS = 8
RET_CHUNK = 256
RET_HEADS_PER_STEP = 8
SWA_HEAD_DIM = 64
SWA_KV_HEADS = 4
SWA_WINDOW = 128
SWA_LOOKAHEAD = 3
SWA_BLOCKS_PER_ITER = 4

V7X_VMEM_BYTES = 64 * 1024 * 1024
VMEM_TILE_BUDGET = 50 * 1024 * 1024
VMEM_HEADROOM = 8 * 1024 * 1024
VMEM_LIMIT_BYTES = V7X_VMEM_BYTES - 8 * 1024 * 1024
V7X_MXU_COLS = 256
NORM_ROWS = 128
NORM_ALIGN = 32
PLE_COLS = 512
ROW_TILES = (1024, 512, 256, 128)


def _pick(n, prefs):
    for t in prefs:
        if n % t == 0:
            return t
    return n


def _params(*sem, vmem_bytes=None):
    limit = VMEM_LIMIT_BYTES if vmem_bytes is None else min(vmem_bytes + VMEM_HEADROOM,
                                                            V7X_VMEM_BYTES - 2 * 1024 * 1024)
    return pltpu.CompilerParams(dimension_semantics=sem, vmem_limit_bytes=limit)


def _choose_tiles(m, n, vmem_bytes):
    best = None
    for tm in ROW_TILES:
        if m % tm:
            continue
        for parts in range(1, n // V7X_MXU_COLS + 1):
            tn = n // parts
            if n % parts or tn % V7X_MXU_COLS:
                continue
            need = vmem_bytes(tm, tn)
            if need > VMEM_TILE_BUDGET:
                continue
            key = (tm * tn, tn == n, tm)
            if best is None or key > best[0]:
                best = (key, tm, tn, need)
    if best is None:
        raise ValueError("no matmul tile fits VMEM")
    return best[1:]


def _weight_spec(block, index_map, resident):
    if resident:
        return pl.BlockSpec(block, index_map, pipeline_mode=pl.Buffered(1))
    return pl.BlockSpec(block, index_map)


def _rmsnorm_rows(x, gain):
    inv = lax.rsqrt(jnp.mean(x * x, axis=-1, keepdims=True) + NORM_EPS)
    return x * inv * gain


def _norm_share(x_ref, gain_ref, h_ref, step, n_steps):
    tm = x_ref.shape[0]
    rows = -(-tm // n_steps)
    rows = min(tm, -(-rows // NORM_ALIGN) * NORM_ALIGN)
    start = pl.multiple_of(jnp.minimum(step * rows, tm - rows), NORM_ALIGN)
    for r in range(0, rows, NORM_ROWS):
        sl = pl.ds(start + r, min(NORM_ROWS, rows - r))
        h_ref[sl, :] = _rmsnorm_rows(x_ref[sl, :], gain_ref[...]).astype(BF16)


def _lagged_tiles(norm_into, matmul_from, bufs, n_tiles):
    i = pl.program_id(0)

    @pl.when(i == 0)
    def _():
        norm_into(bufs[0])

    for parity in (0, 1):
        @pl.when(jnp.logical_and(jnp.logical_and(i > 0, i < n_tiles), i % 2 == parity))
        def _(parity=parity):
            matmul_from(bufs[1 - parity])
            norm_into(bufs[parity])

    @pl.when(i == n_tiles)
    def _():
        matmul_from(bufs[(n_tiles - 1) % 2])


def _lagged_index_maps(n_tiles, layer):
    x_map = lambda i, j: (jnp.minimum(i, n_tiles - 1), 0)
    col = lambda i, j: jnp.where(i == 0, 0, j)
    w_map = lambda shift: (lambda i, j: (layer, 0, col(i, j) + shift))
    o_map = lambda i, j: (jnp.maximum(i - 1, 0), col(i, j))
    return x_map, w_map, o_map


def _norm_matmul_kernel(x_ref, gain_ref, w_ref, o_ref, ha_ref, hb_ref, *, n_tiles, n_cols):
    col = pl.program_id(1)

    def norm_into(h_ref):
        _norm_share(x_ref, gain_ref, h_ref, col, n_cols)

    def matmul_from(h_ref):
        o_ref[...] = jnp.dot(h_ref[...], w_ref[...], preferred_element_type=F32).astype(o_ref.dtype)

    _lagged_tiles(norm_into, matmul_from, (ha_ref, hb_ref), n_tiles)


def norm_matmul(x, gain, w, layer, *, tm=None, tn=None):
    m, d = x.shape
    n = w.shape[-1]

    def vmem_bytes(tm, tn):
        w_bufs = 1 if tn == n else 2
        return 2 * tm * d * 4 + 2 * tm * d * 2 + w_bufs * d * tn * 2 + 2 * tm * tn * 2

    if tm is None or tn is None:
        tm, tn, need = _choose_tiles(m, n, vmem_bytes)
    else:
        need = vmem_bytes(tm, tn)
    n_tiles, n_cols = m // tm, n // tn
    x_map, w_map, o_map = _lagged_index_maps(n_tiles, layer)
    return pl.pallas_call(
        functools.partial(_norm_matmul_kernel, n_tiles=n_tiles, n_cols=n_cols),
        grid=(n_tiles + 1, n_cols),
        in_specs=[
            pl.BlockSpec((tm, d), x_map),
            pl.BlockSpec((1, d), lambda i, j: (0, 0)),
            _weight_spec((None, d, tn), w_map(0), tn == n),
        ],
        out_specs=pl.BlockSpec((tm, tn), o_map),
        out_shape=jax.ShapeDtypeStruct((m, n), BF16),
        scratch_shapes=[pltpu.VMEM((tm, d), BF16), pltpu.VMEM((tm, d), BF16)],
        compiler_params=_params("arbitrary", "arbitrary", vmem_bytes=need),
        name="norm_matmul",
    )(x, gain, w)


def _norm_swiglu_kernel(x_ref, gain_ref, w_ref, o_ref, ha_ref, hb_ref, *, n_tiles, n_cols):
    col = pl.program_id(1)
    tn = o_ref.shape[1]

    def norm_into(h_ref):
        _norm_share(x_ref, gain_ref, h_ref, col, n_cols)

    def matmul_from(h_ref):
        both = jnp.dot(h_ref[...], w_ref[...], preferred_element_type=F32)
        o_ref[...] = (jax.nn.silu(both[:, :tn]) * both[:, tn:]).astype(o_ref.dtype)

    _lagged_tiles(norm_into, matmul_from, (ha_ref, hb_ref), n_tiles)


def _swiglu_vmem_bytes(d):
    return lambda tm, tn: 2 * tm * d * 4 + 2 * tm * d * 2 + 2 * d * 2 * tn * 2 + 2 * tm * tn * 2


def swiglu_tiles(m, d, hd):
    return _choose_tiles(m, hd, _swiglu_vmem_bytes(d))[:2]


def interleave_gate_up(w_in, tn):
    layers, d, two_hd = w_in.shape
    tiles = two_hd // (2 * tn)
    return w_in.reshape(layers, d, 2, tiles, tn).transpose(0, 1, 3, 2, 4).reshape(layers, d, two_hd)


def norm_swiglu(x, gain, w_in, layer, *, tm, tn):
    m, d = x.shape
    hd = w_in.shape[-1] // 2
    need = _swiglu_vmem_bytes(d)(tm, tn)
    n_tiles, n_cols = m // tm, hd // tn
    x_map, w_map, o_map = _lagged_index_maps(n_tiles, layer)
    return pl.pallas_call(
        functools.partial(_norm_swiglu_kernel, n_tiles=n_tiles, n_cols=n_cols),
        grid=(n_tiles + 1, n_cols),
        in_specs=[
            pl.BlockSpec((tm, d), x_map),
            pl.BlockSpec((1, d), lambda i, j: (0, 0)),
            pl.BlockSpec((None, d, 2 * tn), w_map(0)),
        ],
        out_specs=pl.BlockSpec((tm, tn), o_map),
        out_shape=jax.ShapeDtypeStruct((m, hd), BF16),
        scratch_shapes=[pltpu.VMEM((tm, d), BF16), pltpu.VMEM((tm, d), BF16)],
        compiler_params=_params("arbitrary", "arbitrary", vmem_bytes=need),
        name="norm_swiglu",
    )(x, gain, w_in)


def _matmul_residual_kernel(a_ref, w_ref, x_ref, o_ref):
    o_ref[...] = x_ref[...] + jnp.dot(a_ref[...], w_ref[...], preferred_element_type=F32)


def matmul_residual(a, w, x, layer, *, tm=None, tn=None):
    m, k = a.shape
    n = w.shape[-1]

    def vmem_bytes(tm, tn):
        w_bufs = 1 if tn == n else 2
        return 2 * tm * k * 2 + w_bufs * k * tn * 2 + 2 * 2 * tm * tn * 4

    if tm is None or tn is None:
        tm, tn, need = _choose_tiles(m, n, vmem_bytes)
    else:
        need = vmem_bytes(tm, tn)
    return pl.pallas_call(
        _matmul_residual_kernel,
        grid=(m // tm, n // tn),
        in_specs=[
            pl.BlockSpec((tm, k), lambda i, j: (i, 0)),
            _weight_spec((None, k, tn), lambda i, j: (layer, 0, j), tn == n),
            pl.BlockSpec((tm, tn), lambda i, j: (i, j)),
        ],
        out_specs=pl.BlockSpec((tm, tn), lambda i, j: (i, j)),
        out_shape=jax.ShapeDtypeStruct((m, n), F32),
        compiler_params=_params("parallel", "arbitrary", vmem_bytes=need),
        name="matmul_residual",
    )(a, w, x)


def _ple_kernel(xn_ref, xr_ref, gain_ref, wg_ref, p_ref, wp_ref, fgain_ref, o_ref, ha_ref, hb_ref,
                *, n_tiles, final):
    tm, d = o_ref.shape

    def norm_into(h_ref):
        _norm_share(xn_ref, gain_ref, h_ref, 0, 1)

    def matmul_from(h_ref):
        h = h_ref[...]
        pb = p_ref[...].astype(BF16)
        for c in range(0, d, PLE_COLS):
            cols = slice(c, c + PLE_COLS)
            gate = jax.nn.sigmoid(jnp.dot(h, wg_ref[:, cols], preferred_element_type=F32))
            pp = jnp.dot(pb, wp_ref[:, cols], preferred_element_type=F32)
            o_ref[:, cols] = xr_ref[:, cols] + gate * pp
        if final:
            for r in range(0, tm, NORM_ROWS):
                rows = slice(r, r + NORM_ROWS)
                o_ref[rows, :] = _rmsnorm_rows(o_ref[rows, :], fgain_ref[...])

    _lagged_tiles(norm_into, matmul_from, (ha_ref, hb_ref), n_tiles)


def ple_update(x, gain, w_gate, p, w_proj, final_gain, layer, *, final, tm=None):
    m, d = x.shape
    pd = p.shape[-1]
    tm = tm or _pick(m, (512, 256, 128))
    n_tiles = m // tm
    need = 3 * 2 * tm * d * 4 + d * d * 2 + pd * d * 2 + 2 * tm * pd * 4 + 2 * tm * d * 2
    ahead = lambda i: (jnp.minimum(i, n_tiles - 1), 0)
    behind = lambda i: (jnp.maximum(i - 1, 0), 0)
    resident = lambda i: (layer, 0, 0)
    return pl.pallas_call(
        functools.partial(_ple_kernel, n_tiles=n_tiles, final=final),
        grid=(n_tiles + 1,),
        in_specs=[
            pl.BlockSpec((tm, d), ahead),
            pl.BlockSpec((tm, d), behind),
            pl.BlockSpec((1, d), lambda i: (0, 0)),
            _weight_spec((None, d, d), resident, True),
            pl.BlockSpec((None, tm, pd), lambda i: (layer, jnp.maximum(i - 1, 0), 0)),
            _weight_spec((None, pd, d), resident, True),
            pl.BlockSpec((1, d), lambda i: (0, 0)),
        ],
        out_specs=pl.BlockSpec((tm, d), behind),
        out_shape=jax.ShapeDtypeStruct((m, d), F32),
        scratch_shapes=[pltpu.VMEM((tm, d), BF16), pltpu.VMEM((tm, d), BF16)],
        compiler_params=_params("arbitrary", vmem_bytes=need),
        name="ple_update",
    )(x, x, gain, w_gate, p, w_proj, final_gain)


def _ret_core_kernel(q_ref, k_ref, v_ref, g_ref, dint_ref, qd_ref, kd_ref, cd_ref, gn_ref,
                     o_ref, state_ref, *, heads, chunk):
    dk = q_ref.shape[1] // heads
    dv = v_ref.shape[1] // heads

    @pl.when(pl.program_id(2) == 0)
    def _():
        state_ref[...] = jnp.zeros_like(state_ref)

    def body(ci, carry):
        sl = pl.ds(pl.multiple_of(ci * chunk, chunk), chunk)
        for hh in range(heads):
            ck = slice(hh * dk, (hh + 1) * dk)
            cv = slice(hh * dv, (hh + 1) * dv)
            q = q_ref[sl, ck]
            k = k_ref[sl, ck]
            v = v_ref[sl, cv]
            state = state_ref[hh]
            scores = lax.dot_general(q, k, (((1,), (1,)), ((), ())), preferred_element_type=F32)
            scores = (scores * dint_ref[hh]).astype(BF16)
            q_dec = (q.astype(F32) * qd_ref[hh]).astype(BF16)
            o = (jnp.dot(scores, v, preferred_element_type=F32)
                 + jnp.dot(q_dec, state.astype(BF16), preferred_element_type=F32))
            k_dec = (k.astype(F32) * kd_ref[hh]).astype(BF16)
            state_ref[hh] = state * cd_ref[hh] + lax.dot_general(
                k_dec, v, (((0,), (0,)), ((), ())), preferred_element_type=F32)
            mu = jnp.mean(o, axis=-1, keepdims=True)
            dev = o - mu
            var = jnp.mean(dev * dev, axis=-1, keepdims=True)
            yn = dev * lax.rsqrt(var + NORM_EPS) * gn_ref[hh]
            o_ref[sl, cv] = (jax.nn.silu(g_ref[sl, cv].astype(F32)) * yn).astype(o_ref.dtype)
        return carry

    lax.fori_loop(0, q_ref.shape[0] // chunk, body, 0)


def _retention_decays(c, dk, dv):
    h = RET_HEADS
    log_g = jnp.log1p(-jnp.exp2(-5.0 - jnp.arange(h, dtype=F32)))
    pos = jnp.arange(c, dtype=F32)
    diff = pos[:, None] - pos[None, :]
    causal = diff >= 0
    decay_intra = jnp.where(causal, jnp.exp(log_g[:, None, None] * jnp.where(causal, diff, 0.0)), 0.0)
    q_decay = jnp.exp(log_g[:, None] * (pos + 1.0))[:, :, None]
    k_decay = jnp.exp(log_g[:, None] * (c - 1.0 - pos))[:, :, None]
    chunk_decay = jnp.exp(log_g * c)[:, None, None]
    k_scale = dk ** -0.5
    return (decay_intra * k_scale,
            jnp.broadcast_to(q_decay, (h, c, dk)),
            jnp.broadcast_to(k_decay * k_scale, (h, c, dk)),
            jnp.broadcast_to(chunk_decay, (h, 1, dv)))


def retention_core(proj, gn_gain, *, ts=None, heads=RET_HEADS_PER_STEP, chunk=RET_CHUNK):
    b, s, width = proj.shape
    h = RET_HEADS
    dk = width // (6 * h)
    dv = 2 * dk
    ts = ts or _pick(s, (512, 256, 128))
    chunk = min(chunk, ts)
    hg = h // heads
    dint, qd, kd, cd = _retention_decays(chunk, dk, dv)
    head_table = lambda last: pl.BlockSpec((heads,) + last, lambda bi, hi, ti: (hi, 0, 0))
    return pl.pallas_call(
        functools.partial(_ret_core_kernel, heads=heads, chunk=chunk),
        grid=(b, hg, s // ts),
        in_specs=[
            pl.BlockSpec((None, ts, heads * dk), lambda bi, hi, ti: (bi, ti, hi)),
            pl.BlockSpec((None, ts, heads * dk), lambda bi, hi, ti: (bi, ti, hg + hi)),
            pl.BlockSpec((None, ts, heads * dv), lambda bi, hi, ti: (bi, ti, hg + hi)),
            pl.BlockSpec((None, ts, heads * dv), lambda bi, hi, ti: (bi, ti, 2 * hg + hi)),
            head_table((chunk, chunk)),
            head_table((chunk, dk)),
            head_table((chunk, dk)),
            head_table((1, dv)),
            head_table((1, dv)),
        ],
        out_specs=pl.BlockSpec((None, ts, heads * dv), lambda bi, hi, ti: (bi, ti, hi)),
        out_shape=jax.ShapeDtypeStruct((b, s, h * dv), BF16),
        scratch_shapes=[pltpu.VMEM((heads, dk, dv), F32)],
        compiler_params=_params("parallel", "parallel", "arbitrary"),
        name="retention_core",
    )(proj, proj, proj, proj, dint, qd, kd, cd, gn_gain)


def _swa_core_kernel(sinks_ref, q_ref, kp_ref, kc_ref, vp_ref, vc_ref, bias_ref, o_ref,
                     kb_ref, vb_ref, *, n_heads):
    w, hd = SWA_WINDOW, SWA_HEAD_DIM
    group = n_heads // SWA_KV_HEADS
    ts = q_ref.shape[0]
    q_scale = hd ** -0.5
    kb_ref[0:w, :] = kp_ref[...] * q_scale
    kb_ref[w:, :] = kc_ref[...] * q_scale
    vb_ref[0:w, :] = vp_ref[...]
    vb_ref[w:, :] = vc_ref[...]

    row = lax.broadcasted_iota(jnp.int32, (w, w), 0)
    lane = lax.broadcasted_iota(jnp.int32, (w, w), 1)
    upper = lane > row
    left = lane < hd
    zero = jnp.zeros((w, w), BF16)
    upper2 = jnp.concatenate([upper, upper], axis=1)
    zero2 = jnp.zeros((w, 2 * w), BF16)
    ones_diag = jnp.concatenate([jnp.where(left, 1.0, 0.0), jnp.where(left, 0.0, 1.0)],
                                axis=0).astype(BF16)
    first_tile = pl.program_id(1) == 0
    nt = (((1,), (1,)), ((), ()))

    def split(slab):
        lhs = lax.broadcasted_iota(jnp.int32, slab.shape, 1) < hd
        nil = jnp.zeros(slab.shape, BF16)
        swapped = pltpu.roll(slab, hd, axis=1)
        return ((jnp.where(lhs, slab, nil), jnp.where(lhs, nil, swapped)),
                (jnp.where(lhs, swapped, nil), jnp.where(lhs, nil, slab)))

    def block_stages(blk):
        row0 = pl.multiple_of(blk * w, w)
        fidx = jnp.logical_and(first_tile, blk == 0).astype(jnp.int32)
        rows = pl.ds(row0, w)
        prev_rows = pl.ds(row0, w)
        cur_rows = pl.ds(row0 + w, w)
        kband, vprev, vcur = [], [], []
        for slab_i in range(SWA_KV_HEADS // 2):
            cols = slice(slab_i * 2 * hd, (slab_i + 1) * 2 * hd)
            kband.extend(split(kb_ref[pl.ds(row0, 2 * w), cols]))
            vprev.extend(jnp.concatenate([jnp.concatenate(t, axis=0), ones_diag], axis=1)
                         for t in split(vb_ref[prev_rows, cols]))
            vcur.extend(jnp.concatenate([jnp.concatenate(t, axis=0), ones_diag], axis=1)
                        for t in split(vb_ref[cur_rows, cols]))

        def scores(pair):
            kv = (2 * pair) // group
            qp = q_ref[rows, 2 * pair * hd:(2 * pair + 2) * hd]
            out = []
            for side in range(2):
                s2 = lax.dot_general(qp, kband[kv][side], nt, preferred_element_type=F32)
                out.append(jnp.where(upper, s2[:, :w], s2[:, w:]) + bias_ref[fidx, 2 * pair + side])
            return out

        def softmax_numerators(pair, s_pair):
            out = []
            for side in range(2):
                sink = sinks_ref[2 * pair + side]
                m = jnp.maximum(jnp.max(s_pair[side], axis=-1, keepdims=True), sink)
                out.append((jnp.exp(s_pair[side] - m).astype(BF16), jnp.exp(sink - m)))
            return out

        def weighted_values(pair, e_pair):
            kv = (2 * pair) // group
            e = jnp.concatenate([e_pair[0][0], e_pair[1][0]], axis=1)
            both = (jnp.dot(jnp.where(upper2, e, zero2), vprev[kv], preferred_element_type=F32)
                    + jnp.dot(jnp.where(upper2, zero2, e), vcur[kv], preferred_element_type=F32))
            acc = both[:, :w]
            den = both[:, w:] + jnp.where(left, e_pair[0][1], e_pair[1][1])
            o_ref[rows, 2 * pair * hd:(2 * pair + 2) * hd] = (acc * (1.0 / den)).astype(o_ref.dtype)

        return scores, softmax_numerators, weighted_values

    def body(n, carry):
        stages = [block_stages(n * SWA_BLOCKS_PER_ITER + u) for u in range(SWA_BLOCKS_PER_ITER)]
        n_pairs = n_heads // 2
        items = [(u, pair) for u in range(SWA_BLOCKS_PER_ITER) for pair in range(n_pairs)]
        s_tiles, e_tiles = {}, {}
        for t in range(len(items) + 2 * SWA_LOOKAHEAD):
            if t < len(items):
                u, pair = items[t]
                s_tiles[t] = stages[u][0](pair)
            t1 = t - SWA_LOOKAHEAD
            if 0 <= t1 < len(items):
                u, pair = items[t1]
                e_tiles[t1] = stages[u][1](pair, s_tiles.pop(t1))
            t2 = t1 - SWA_LOOKAHEAD
            if 0 <= t2 < len(items):
                u, pair = items[t2]
                stages[u][2](pair, e_tiles.pop(t2))
        return carry

    lax.fori_loop(0, ts // (w * SWA_BLOCKS_PER_ITER), body, 0)


def _swa_bias(n_heads):
    w = SWA_WINDOW
    slopes = jnp.exp2(-8.0 * (jnp.arange(n_heads, dtype=F32) + 1.0) / n_heads)
    qi = jnp.arange(w)[:, None]
    kc = jnp.arange(w)[None, :]
    upper = kc > qi
    rel = jnp.where(upper, qi - kc + w, qi - kc).astype(F32)
    bias = -slopes[:, None, None] * rel
    return jnp.stack([bias, jnp.where(upper, -jnp.inf, bias)])


def swa_core(proj, sinks, *, ts=None):
    b, s, width = proj.shape
    w = SWA_WINDOW
    kvw = SWA_KV_HEADS * SWA_HEAD_DIM
    qw = width - 2 * kvw
    n_heads = qw // SWA_HEAD_DIM
    ts = ts or _pick(s, (512, 256, 128))
    bpt = ts // w
    prev = lambda col: (lambda bi, ti: (bi, jnp.maximum(ti * bpt - 1, 0), col))
    cur = lambda col: (lambda bi, ti: (bi, ti, col))
    return pl.pallas_call(
        functools.partial(_swa_core_kernel, n_heads=n_heads),
        grid=(b, s // ts),
        in_specs=[
            pl.BlockSpec(memory_space=pltpu.SMEM),
            pl.BlockSpec((None, ts, qw), lambda bi, ti: (bi, ti, 0)),
            pl.BlockSpec((None, w, kvw), prev(qw // kvw)),
            pl.BlockSpec((None, ts, kvw), cur(qw // kvw)),
            pl.BlockSpec((None, w, kvw), prev(qw // kvw + 1)),
            pl.BlockSpec((None, ts, kvw), cur(qw // kvw + 1)),
            pl.BlockSpec((2, n_heads, w, w), lambda bi, ti: (0, 0, 0, 0)),
        ],
        out_specs=pl.BlockSpec((None, ts, qw), lambda bi, ti: (bi, ti, 0)),
        out_shape=jax.ShapeDtypeStruct((b, s, qw), BF16),
        scratch_shapes=[pltpu.VMEM((ts + w, kvw), BF16), pltpu.VMEM((ts + w, kvw), BF16)],
        compiler_params=_params("parallel", "arbitrary"),
        name="swa_core",
    )(sinks, proj, proj, proj, proj, proj, _swa_bias(n_heads))


def kernel(x, p, norm_mix, norm_ffn, norm_ple, norm_final, ret_w_in, ret_gn, ret_w_out,
           swa_w_in, swa_sinks, swa_w_out, ffn_w_in, ffn_w_out, ple_w_proj, ple_w_gate):
    b, s, d = x.shape
    depth = p.shape[0]
    m = b * s
    ret_w_in, ret_w_out, swa_w_in, swa_w_out, ffn_w_in, ffn_w_out, ple_w_proj, ple_w_gate = (
        a.astype(BF16) for a in (ret_w_in, ret_w_out, swa_w_in, swa_w_out, ffn_w_in, ffn_w_out,
                                 ple_w_proj, ple_w_gate))
    ffn_tm, ffn_tn = swiglu_tiles(m, d, ffn_w_out.shape[1])
    ffn_w_in = interleave_gate_up(ffn_w_in, ffn_tn)
    p = p.reshape(depth, m, p.shape[-1])
    x = x.reshape(m, d)
    final_gain = norm_final[None, :]
    for i in range(depth):
        j = i // 2
        if i % 2 == 0:
            proj = norm_matmul(x, norm_mix[i][None, :], ret_w_in, j)
            y = retention_core(proj.reshape(b, s, -1), ret_gn[j][:, None, :])
            x = matmul_residual(y.reshape(m, -1), ret_w_out, x, j)
        else:
            proj = norm_matmul(x, norm_mix[i][None, :], swa_w_in, j)
            y = swa_core(proj.reshape(b, s, -1), swa_sinks[j])
            x = matmul_residual(y.reshape(m, -1), swa_w_out, x, j)
        hidden = norm_swiglu(x, norm_ffn[i][None, :], ffn_w_in, i, tm=ffn_tm, tn=ffn_tn)
        x = matmul_residual(hidden, ffn_w_out, x, i)
        x = ple_update(x, norm_ple[i][None, :], ple_w_gate, p, ple_w_proj, final_gain, i,
                       final=(i == depth - 1))
    return x.reshape(b, s, d)
```

```python
import functools

import jax
import jax.numpy as jnp
from jax import lax
from jax.experimental import pallas as pl
from jax.experimental.pallas import tpu as pltpu

F32 = jnp.float32
BF16 = jnp.bfloat16

NORM_EPS = 1e-6

RET_HEADS = 8
RET_CHUNK = 256
RET_HEADS_PER_STEP = 8
SWA_HEAD_DIM = 64
SWA_KV_HEADS = 4
SWA_WINDOW = 128
SWA_LOOKAHEAD = 3
SWA_BLOCKS_PER_ITER = 4

V7X_VMEM_BYTES = 64 * 1024 * 1024
VMEM_TILE_BUDGET = 50 * 1024 * 1024
VMEM_HEADROOM = 8 * 1024 * 1024
VMEM_LIMIT_BYTES = V7X_VMEM_BYTES - 8 * 1024 * 1024
V7X_MXU_COLS = 256
NORM_ROWS = 128
NORM_ALIGN = 32
PLE_COLS = 512
ROW_TILES = (1024, 512, 256, 128)


def _pick(n, prefs):
    for t in prefs:
        if n % t == 0:
            return t
    return n


def _params(*sem, vmem_bytes=None):
    limit = VMEM_LIMIT_BYTES if vmem_bytes is None else min(vmem_bytes + VMEM_HEADROOM,
                                                            V7X_VMEM_BYTES - 2 * 1024 * 1024)
    return pltpu.CompilerParams(dimension_semantics=sem, vmem_limit_bytes=limit)


def _choose_tiles(m, n, vmem_bytes):
    best = None
    for tm in ROW_TILES:
        if m % tm:
            continue
        for parts in range(1, n // V7X_MXU_COLS + 1):
            tn = n // parts
            if n % parts or tn % V7X_MXU_COLS:
                continue
            need = vmem_bytes(tm, tn)
            if need > VMEM_TILE_BUDGET:
                continue
            key = (tm * tn, tn == n, tm)
            if best is None or key > best[0]:
                best = (key, tm, tn, need)
    if best is None:
        raise ValueError("no matmul tile fits VMEM")
    return best[1:]


def _weight_spec(block, index_map, resident):
    if resident:
        return pl.BlockSpec(block, index_map, pipeline_mode=pl.Buffered(1))
    return pl.BlockSpec(block, index_map)


def _rmsnorm_rows(x, gain):
    inv = lax.rsqrt(jnp.mean(x * x, axis=-1, keepdims=True) + NORM_EPS)
    return x * inv * gain


def _norm_share(x_ref, gain_ref, h_ref, step, n_steps):
    tm = x_ref.shape[0]
    rows = -(-tm // n_steps)
    rows = min(tm, -(-rows // NORM_ALIGN) * NORM_ALIGN)
    start = pl.multiple_of(jnp.minimum(step * rows, tm - rows), NORM_ALIGN)
    for r in range(0, rows, NORM_ROWS):
        sl = pl.ds(start + r, min(NORM_ROWS, rows - r))
        h_ref[sl, :] = _rmsnorm_rows(x_ref[sl, :], gain_ref[...]).astype(BF16)


def _lagged_tiles(norm_into, matmul_from, bufs, n_tiles):
    i = pl.program_id(0)

    @pl.when(i == 0)
    def _():
        norm_into(bufs[0])

    for parity in (0, 1):
        @pl.when(jnp.logical_and(jnp.logical_and(i > 0, i < n_tiles), i % 2 == parity))
        def _(parity=parity):
            matmul_from(bufs[1 - parity])
            norm_into(bufs[parity])

    @pl.when(i == n_tiles)
    def _():
        matmul_from(bufs[(n_tiles - 1) % 2])


def _lagged_index_maps(n_tiles, layer):
    x_map = lambda i, j: (jnp.minimum(i, n_tiles - 1), 0)
    col = lambda i, j: jnp.where(i == 0, 0, j)
    w_map = lambda shift: (lambda i, j: (layer, 0, col(i, j) + shift))
    o_map = lambda i, j: (jnp.maximum(i - 1, 0), col(i, j))
    return x_map, w_map, o_map


def _norm_matmul_kernel(x_ref, gain_ref, w_ref, o_ref, ha_ref, hb_ref, *, n_tiles, n_cols):
    col = pl.program_id(1)

    def norm_into(h_ref):
        _norm_share(x_ref, gain_ref, h_ref, col, n_cols)

    def matmul_from(h_ref):
        o_ref[...] = jnp.dot(h_ref[...], w_ref[...], preferred_element_type=F32).astype(o_ref.dtype)

    _lagged_tiles(norm_into, matmul_from, (ha_ref, hb_ref), n_tiles)


def norm_matmul(x, gain, w, layer, *, tm=None, tn=None):
    m, d = x.shape
    n = w.shape[-1]

    def vmem_bytes(tm, tn):
        w_bufs = 1 if tn == n else 2
        return 2 * tm * d * 4 + 2 * tm * d * 2 + w_bufs * d * tn * 2 + 2 * tm * tn * 2

    if tm is None or tn is None:
        tm, tn, need = _choose_tiles(m, n, vmem_bytes)
    else:
        need = vmem_bytes(tm, tn)
    n_tiles, n_cols = m // tm, n // tn
    x_map, w_map, o_map = _lagged_index_maps(n_tiles, layer)
    return pl.pallas_call(
        functools.partial(_norm_matmul_kernel, n_tiles=n_tiles, n_cols=n_cols),
        grid=(n_tiles + 1, n_cols),
        in_specs=[
            pl.BlockSpec((tm, d), x_map),
            pl.BlockSpec((1, d), lambda i, j: (0, 0)),
            _weight_spec((None, d, tn), w_map(0), tn == n),
        ],
        out_specs=pl.BlockSpec((tm, tn), o_map),
        out_shape=jax.ShapeDtypeStruct((m, n), BF16),
        scratch_shapes=[pltpu.VMEM((tm, d), BF16), pltpu.VMEM((tm, d), BF16)],
        compiler_params=_params("arbitrary", "arbitrary", vmem_bytes=need),
        name="norm_matmul",
    )(x, gain, w)


def _norm_swiglu_kernel(x_ref, gain_ref, wg_ref, wu_ref, o_ref, ha_ref, hb_ref, *, n_tiles, n_cols):
    col = pl.program_id(1)

    def norm_into(h_ref):
        _norm_share(x_ref, gain_ref, h_ref, col, n_cols)

    def matmul_from(h_ref):
        h = h_ref[...]
        gate = jnp.dot(h, wg_ref[...], preferred_element_type=F32)
        up = jnp.dot(h, wu_ref[...], preferred_element_type=F32)
        o_ref[...] = (jax.nn.silu(gate) * up).astype(o_ref.dtype)

    _lagged_tiles(norm_into, matmul_from, (ha_ref, hb_ref), n_tiles)


def norm_swiglu(x, gain, w_in, layer, *, tm=None, tn=None):
    m, d = x.shape
    hd = w_in.shape[-1] // 2

    def vmem_bytes(tm, tn):
        return 2 * tm * d * 4 + 2 * tm * d * 2 + 2 * 2 * d * tn * 2 + 2 * tm * tn * 2

    if tm is None or tn is None:
        tm, tn, need = _choose_tiles(m, hd, vmem_bytes)
    else:
        need = vmem_bytes(tm, tn)
    n_tiles, n_cols = m // tm, hd // tn
    x_map, w_map, o_map = _lagged_index_maps(n_tiles, layer)
    return pl.pallas_call(
        functools.partial(_norm_swiglu_kernel, n_tiles=n_tiles, n_cols=n_cols),
        grid=(n_tiles + 1, n_cols),
        in_specs=[
            pl.BlockSpec((tm, d), x_map),
            pl.BlockSpec((1, d), lambda i, j: (0, 0)),
            pl.BlockSpec((None, d, tn), w_map(0)),
            pl.BlockSpec((None, d, tn), w_map(n_cols)),
        ],
        out_specs=pl.BlockSpec((tm, tn), o_map),
        out_shape=jax.ShapeDtypeStruct((m, hd), BF16),
        scratch_shapes=[pltpu.VMEM((tm, d), BF16), pltpu.VMEM((tm, d), BF16)],
        compiler_params=_params("arbitrary", "arbitrary", vmem_bytes=need),
        name="norm_swiglu",
    )(x, gain, w_in, w_in)


def _matmul_residual_kernel(a_ref, w_ref, x_ref, o_ref):
    o_ref[...] = x_ref[...] + jnp.dot(a_ref[...], w_ref[...], preferred_element_type=F32)


def matmul_residual(a, w, x, layer, *, tm=None, tn=None):
    m, k = a.shape
    n = w.shape[-1]

    def vmem_bytes(tm, tn):
        w_bufs = 1 if tn == n else 2
        return 2 * tm * k * 2 + w_bufs * k * tn * 2 + 2 * 2 * tm * tn * 4

    if tm is None or tn is None:
        tm, tn, need = _choose_tiles(m, n, vmem_bytes)
    else:
        need = vmem_bytes(tm, tn)
    return pl.pallas_call(
        _matmul_residual_kernel,
        grid=(m // tm, n // tn),
        in_specs=[
            pl.BlockSpec((tm, k), lambda i, j: (i, 0)),
            _weight_spec((None, k, tn), lambda i, j: (layer, 0, j), tn == n),
            pl.BlockSpec((tm, tn), lambda i, j: (i, j)),
        ],
        out_specs=pl.BlockSpec((tm, tn), lambda i, j: (i, j)),
        out_shape=jax.ShapeDtypeStruct((m, n), F32),
        compiler_params=_params("parallel", "arbitrary", vmem_bytes=need),
        name="matmul_residual",
    )(a, w, x)


def _ple_kernel(xn_ref, xr_ref, gain_ref, wg_ref, p_ref, wp_ref, fgain_ref, o_ref, ha_ref, hb_ref,
                *, n_tiles, final):
    tm, d = o_ref.shape

    def norm_into(h_ref):
        _norm_share(xn_ref, gain_ref, h_ref, 0, 1)

    def matmul_from(h_ref):
        h = h_ref[...]
        pb = p_ref[...].astype(BF16)
        for c in range(0, d, PLE_COLS):
            cols = slice(c, c + PLE_COLS)
            gate = jax.nn.sigmoid(jnp.dot(h, wg_ref[:, cols], preferred_element_type=F32))
            pp = jnp.dot(pb, wp_ref[:, cols], preferred_element_type=F32)
            o_ref[:, cols] = xr_ref[:, cols] + gate * pp
        if final:
            for r in range(0, tm, NORM_ROWS):
                rows = slice(r, r + NORM_ROWS)
                o_ref[rows, :] = _rmsnorm_rows(o_ref[rows, :], fgain_ref[...])

    _lagged_tiles(norm_into, matmul_from, (ha_ref, hb_ref), n_tiles)


def ple_update(x, gain, w_gate, p, w_proj, final_gain, layer, *, final, tm=None):
    m, d = x.shape
    pd = p.shape[-1]
    tm = tm or _pick(m, (512, 256, 128))
    n_tiles = m // tm
    need = 3 * 2 * tm * d * 4 + d * d * 2 + pd * d * 2 + 2 * tm * pd * 4 + 2 * tm * d * 2
    ahead = lambda i: (jnp.minimum(i, n_tiles - 1), 0)
    behind = lambda i: (jnp.maximum(i - 1, 0), 0)
    resident = lambda i: (layer, 0, 0)
    return pl.pallas_call(
        functools.partial(_ple_kernel, n_tiles=n_tiles, final=final),
        grid=(n_tiles + 1,),
        in_specs=[
            pl.BlockSpec((tm, d), ahead),
            pl.BlockSpec((tm, d), behind),
            pl.BlockSpec((1, d), lambda i: (0, 0)),
            _weight_spec((None, d, d), resident, True),
            pl.BlockSpec((None, tm, pd), lambda i: (layer, jnp.maximum(i - 1, 0), 0)),
            _weight_spec((None, pd, d), resident, True),
            pl.BlockSpec((1, d), lambda i: (0, 0)),
        ],
        out_specs=pl.BlockSpec((tm, d), behind),
        out_shape=jax.ShapeDtypeStruct((m, d), F32),
        scratch_shapes=[pltpu.VMEM((tm, d), BF16), pltpu.VMEM((tm, d), BF16)],
        compiler_params=_params("arbitrary", vmem_bytes=need),
        name="ple_update",
    )(x, x, gain, w_gate, p, w_proj, final_gain)


def _ret_core_kernel(q_ref, k_ref, v_ref, g_ref, dint_ref, qd_ref, kd_ref, cd_ref, gn_ref,
                     o_ref, state_ref, *, heads, chunk):
    dk = q_ref.shape[1] // heads
    dv = v_ref.shape[1] // heads

    @pl.when(pl.program_id(2) == 0)
    def _():
        state_ref[...] = jnp.zeros_like(state_ref)

    def body(ci, carry):
        sl = pl.ds(pl.multiple_of(ci * chunk, chunk), chunk)
        for hh in range(heads):
            ck = slice(hh * dk, (hh + 1) * dk)
            cv = slice(hh * dv, (hh + 1) * dv)
            q = q_ref[sl, ck]
            k = k_ref[sl, ck]
            v = v_ref[sl, cv]
            state = state_ref[hh]
            scores = lax.dot_general(q, k, (((1,), (1,)), ((), ())), preferred_element_type=F32)
            scores = (scores * dint_ref[hh]).astype(BF16)
            q_dec = (q.astype(F32) * qd_ref[hh]).astype(BF16)
            o = (jnp.dot(scores, v, preferred_element_type=F32)
                 + jnp.dot(q_dec, state.astype(BF16), preferred_element_type=F32))
            k_dec = (k.astype(F32) * kd_ref[hh]).astype(BF16)
            state_ref[hh] = state * cd_ref[hh] + lax.dot_general(
                k_dec, v, (((0,), (0,)), ((), ())), preferred_element_type=F32)
            mu = jnp.mean(o, axis=-1, keepdims=True)
            dev = o - mu
            var = jnp.mean(dev * dev, axis=-1, keepdims=True)
            yn = dev * lax.rsqrt(var + NORM_EPS) * gn_ref[hh]
            o_ref[sl, cv] = (jax.nn.silu(g_ref[sl, cv].astype(F32)) * yn).astype(o_ref.dtype)
        return carry

    lax.fori_loop(0, q_ref.shape[0] // chunk, body, 0)


def _retention_decays(c, dk, dv):
    h = RET_HEADS
    log_g = jnp.log1p(-jnp.exp2(-5.0 - jnp.arange(h, dtype=F32)))
    pos = jnp.arange(c, dtype=F32)
    diff = pos[:, None] - pos[None, :]
    causal = diff >= 0
    decay_intra = jnp.where(causal, jnp.exp(log_g[:, None, None] * jnp.where(causal, diff, 0.0)), 0.0)
    q_decay = jnp.exp(log_g[:, None] * (pos + 1.0))[:, :, None]
    k_decay = jnp.exp(log_g[:, None] * (c - 1.0 - pos))[:, :, None]
    chunk_decay = jnp.exp(log_g * c)[:, None, None]
    k_scale = dk ** -0.5
    return (decay_intra * k_scale,
            jnp.broadcast_to(q_decay, (h, c, dk)),
            jnp.broadcast_to(k_decay * k_scale, (h, c, dk)),
            jnp.broadcast_to(chunk_decay, (h, 1, dv)))


def retention_core(proj, gn_gain, *, ts=None, heads=RET_HEADS_PER_STEP, chunk=RET_CHUNK):
    b, s, width = proj.shape
    h = RET_HEADS
    dk = width // (6 * h)
    dv = 2 * dk
    ts = ts or _pick(s, (512, 256, 128))
    chunk = min(chunk, ts)
    hg = h // heads
    dint, qd, kd, cd = _retention_decays(chunk, dk, dv)
    head_table = lambda last: pl.BlockSpec((heads,) + last, lambda bi, hi, ti: (hi, 0, 0))
    return pl.pallas_call(
        functools.partial(_ret_core_kernel, heads=heads, chunk=chunk),
        grid=(b, hg, s // ts),
        in_specs=[
            pl.BlockSpec((None, ts, heads * dk), lambda bi, hi, ti: (bi, ti, hi)),
            pl.BlockSpec((None, ts, heads * dk), lambda bi, hi, ti: (bi, ti, hg + hi)),
            pl.BlockSpec((None, ts, heads * dv), lambda bi, hi, ti: (bi, ti, hg + hi)),
            pl.BlockSpec((None, ts, heads * dv), lambda bi, hi, ti: (bi, ti, 2 * hg + hi)),
            head_table((chunk, chunk)),
            head_table((chunk, dk)),
            head_table((chunk, dk)),
            head_table((1, dv)),
            head_table((1, dv)),
        ],
        out_specs=pl.BlockSpec((None, ts, heads * dv), lambda bi, hi, ti: (bi, ti, hi)),
        out_shape=jax.ShapeDtypeStruct((b, s, h * dv), BF16),
        scratch_shapes=[pltpu.VMEM((heads, dk, dv), F32)],
        compiler_params=_params("parallel", "parallel", "arbitrary"),
        name="retention_core",
    )(proj, proj, proj, proj, dint, qd, kd, cd, gn_gain)


def _swa_core_kernel(sinks_ref, q_ref, kp_ref, kc_ref, vp_ref, vc_ref, bias_ref, o_ref,
                     kb_ref, vb_ref, *, n_heads):
    w, hd = SWA_WINDOW, SWA_HEAD_DIM
    group = n_heads // SWA_KV_HEADS
    ts = q_ref.shape[0]
    q_scale = hd ** -0.5
    kb_ref[0:w, :] = kp_ref[...] * q_scale
    kb_ref[w:, :] = kc_ref[...] * q_scale
    vb_ref[0:w, :] = vp_ref[...]
    vb_ref[w:, :] = vc_ref[...]

    row = lax.broadcasted_iota(jnp.int32, (w, w), 0)
    lane = lax.broadcasted_iota(jnp.int32, (w, w), 1)
    upper = lane > row
    left = lane < hd
    zero = jnp.zeros((w, w), BF16)
    upper2 = jnp.concatenate([upper, upper], axis=1)
    zero2 = jnp.zeros((w, 2 * w), BF16)
    ones_diag = jnp.concatenate([jnp.where(left, 1.0, 0.0), jnp.where(left, 0.0, 1.0)],
                                axis=0).astype(BF16)
    first_tile = pl.program_id(1) == 0
    nt = (((1,), (1,)), ((), ()))

    def split(slab):
        lhs = lax.broadcasted_iota(jnp.int32, slab.shape, 1) < hd
        nil = jnp.zeros(slab.shape, BF16)
        swapped = pltpu.roll(slab, hd, axis=1)
        return ((jnp.where(lhs, slab, nil), jnp.where(lhs, nil, swapped)),
                (jnp.where(lhs, swapped, nil), jnp.where(lhs, nil, slab)))

    def block_stages(blk):
        row0 = pl.multiple_of(blk * w, w)
        fidx = jnp.logical_and(first_tile, blk == 0).astype(jnp.int32)
        rows = pl.ds(row0, w)
        prev_rows = pl.ds(row0, w)
        cur_rows = pl.ds(row0 + w, w)
        kband, vprev, vcur = [], [], []
        for slab_i in range(SWA_KV_HEADS // 2):
            cols = slice(slab_i * 2 * hd, (slab_i + 1) * 2 * hd)
            kband.extend(split(kb_ref[pl.ds(row0, 2 * w), cols]))
            vprev.extend(jnp.concatenate([jnp.concatenate(t, axis=0), ones_diag], axis=1)
                         for t in split(vb_ref[prev_rows, cols]))
            vcur.extend(jnp.concatenate([jnp.concatenate(t, axis=0), ones_diag], axis=1)
                        for t in split(vb_ref[cur_rows, cols]))

        def scores(pair):
            kv = (2 * pair) // group
            qp = q_ref[rows, 2 * pair * hd:(2 * pair + 2) * hd]
            out = []
            for side in range(2):
                s2 = lax.dot_general(qp, kband[kv][side], nt, preferred_element_type=F32)
                out.append(jnp.where(upper, s2[:, :w], s2[:, w:]) + bias_ref[fidx, 2 * pair + side])
            return out

        def softmax_numerators(pair, s_pair):
            out = []
            for side in range(2):
                sink = sinks_ref[2 * pair + side]
                m = jnp.maximum(jnp.max(s_pair[side], axis=-1, keepdims=True), sink)
                out.append((jnp.exp(s_pair[side] - m).astype(BF16), jnp.exp(sink - m)))
            return out

        def weighted_values(pair, e_pair):
            kv = (2 * pair) // group
            e = jnp.concatenate([e_pair[0][0], e_pair[1][0]], axis=1)
            both = (jnp.dot(jnp.where(upper2, e, zero2), vprev[kv], preferred_element_type=F32)
                    + jnp.dot(jnp.where(upper2, zero2, e), vcur[kv], preferred_element_type=F32))
            acc = both[:, :w]
            den = both[:, w:] + jnp.where(left, e_pair[0][1], e_pair[1][1])
            o_ref[rows, 2 * pair * hd:(2 * pair + 2) * hd] = (acc * (1.0 / den)).astype(o_ref.dtype)

        return scores, softmax_numerators, weighted_values

    def body(n, carry):
        stages = [block_stages(n * SWA_BLOCKS_PER_ITER + u) for u in range(SWA_BLOCKS_PER_ITER)]
        n_pairs = n_heads // 2
        items = [(u, pair) for u in range(SWA_BLOCKS_PER_ITER) for pair in range(n_pairs)]
        s_tiles, e_tiles = {}, {}
        for t in range(len(items) + 2 * SWA_LOOKAHEAD):
            if t < len(items):
                u, pair = items[t]
                s_tiles[t] = stages[u][0](pair)
            t1 = t - SWA_LOOKAHEAD
            if 0 <= t1 < len(items):
                u, pair = items[t1]
                e_tiles[t1] = stages[u][1](pair, s_tiles.pop(t1))
            t2 = t1 - SWA_LOOKAHEAD
            if 0 <= t2 < len(items):
                u, pair = items[t2]
                stages[u][2](pair, e_tiles.pop(t2))
        return carry

    lax.fori_loop(0, ts // (w * SWA_BLOCKS_PER_ITER), body, 0)


def _swa_bias(n_heads):
    w = SWA_WINDOW
    slopes = jnp.exp2(-8.0 * (jnp.arange(n_heads, dtype=F32) + 1.0) / n_heads)
    qi = jnp.arange(w)[:, None]
    kc = jnp.arange(w)[None, :]
    upper = kc > qi
    rel = jnp.where(upper, qi - kc + w, qi - kc).astype(F32)
    bias = -slopes[:, None, None] * rel
    return jnp.stack([bias, jnp.where(upper, -jnp.inf, bias)])


def swa_core(proj, sinks, *, ts=None):
    b, s, width = proj.shape
    w = SWA_WINDOW
    kvw = SWA_KV_HEADS * SWA_HEAD_DIM
    qw = width - 2 * kvw
    n_heads = qw // SWA_HEAD_DIM
    ts = ts or _pick(s, (512, 256, 128))
    bpt = ts // w
    prev = lambda col: (lambda bi, ti: (bi, jnp.maximum(ti * bpt - 1, 0), col))
    cur = lambda col: (lambda bi, ti: (bi, ti, col))
    return pl.pallas_call(
        functools.partial(_swa_core_kernel, n_heads=n_heads),
        grid=(b, s // ts),
        in_specs=[
            pl.BlockSpec(memory_space=pltpu.SMEM),
            pl.BlockSpec((None, ts, qw), lambda bi, ti: (bi, ti, 0)),
            pl.BlockSpec((None, w, kvw), prev(qw // kvw)),
            pl.BlockSpec((None, ts, kvw), cur(qw // kvw)),
            pl.BlockSpec((None, w, kvw), prev(qw // kvw + 1)),
            pl.BlockSpec((None, ts, kvw), cur(qw // kvw + 1)),
            pl.BlockSpec((2, n_heads, w, w), lambda bi, ti: (0, 0, 0, 0)),
        ],
        out_specs=pl.BlockSpec((None, ts, qw), lambda bi, ti: (bi, ti, 0)),
        out_shape=jax.ShapeDtypeStruct((b, s, qw), BF16),
        scratch_shapes=[pltpu.VMEM((ts + w, kvw), BF16), pltpu.VMEM((ts + w, kvw), BF16)],
        compiler_params=_params("parallel", "arbitrary"),
        name="swa_core",
    )(sinks, proj, proj, proj, proj, proj, _swa_bias(n_heads))


def kernel(x, p, norm_mix, norm_ffn, norm_ple, norm_final, ret_w_in, ret_gn, ret_w_out,
           swa_w_in, swa_sinks, swa_w_out, ffn_w_in, ffn_w_out, ple_w_proj, ple_w_gate):
    b, s, d = x.shape
    depth = p.shape[0]
    m = b * s
    ret_w_in, ret_w_out, swa_w_in, swa_w_out, ffn_w_in, ffn_w_out, ple_w_proj, ple_w_gate = (
        a.astype(BF16) for a in (ret_w_in, ret_w_out, swa_w_in, swa_w_out, ffn_w_in, ffn_w_out,
                                 ple_w_proj, ple_w_gate))
    p = p.reshape(depth, m, p.shape[-1])
    x = x.reshape(m, d)
    final_gain = norm_final[None, :]
    for i in range(depth):
        j = i // 2
        if i % 2 == 0:
            proj = norm_matmul(x, norm_mix[i][None, :], ret_w_in, j)
            y = retention_core(proj.reshape(b, s, -1), ret_gn[j][:, None, :])
            x = matmul_residual(y.reshape(m, -1), ret_w_out, x, j)
        else:
            proj = norm_matmul(x, norm_mix[i][None, :], swa_w_in, j)
            y = swa_core(proj.reshape(b, s, -1), swa_sinks[j])
            x = matmul_residual(y.reshape(m, -1), swa_w_out, x, j)
        hidden = norm_swiglu(x, norm_ffn[i][None, :], ffn_w_in, i)
        x = matmul_residual(hidden, ffn_w_out, x, i)
        x = ple_update(x, norm_ple[i][None, :], ple_w_gate, p, ple_w_proj, final_gain, i,
                       final=(i == depth - 1))
    return x.reshape(b, s, d)
```

```python
import functools

import jax
import jax.numpy as jnp
from jax import lax
from jax.experimental import pallas as pl
from jax.experimental.pallas import tpu as pltpu

F32 = jnp.float32
BF16 = jnp.bfloat16

NORM_EPS = 1e-6

RET_HEADS = 8
RET_CHUNK = 256
RET_HEADS_PER_STEP = 8
SWA_HEAD_DIM = 64
SWA_KV_HEADS = 4
SWA_WINDOW = 128
SWA_LOOKAHEAD = 3
SWA_BLOCKS_PER_ITER = 4

V7X_VMEM_BYTES = 64 * 1024 * 1024
VMEM_TILE_BUDGET = 50 * 1024 * 1024
VMEM_HEADROOM = 8 * 1024 * 1024
VMEM_LIMIT_BYTES = V7X_VMEM_BYTES - 8 * 1024 * 1024
V7X_MXU_COLS = 256
NORM_ROWS = 128
NORM_ALIGN = 32
PLE_COLS = 512
ROW_TILES = (1024, 512, 256, 128)


def _pick(n, prefs):
    for t in prefs:
        if n % t == 0:
            return t
    return n


def _params(*sem, vmem_bytes=None):
    limit = VMEM_LIMIT_BYTES if vmem_bytes is None else min(vmem_bytes + VMEM_HEADROOM,
                                                            V7X_VMEM_BYTES - 2 * 1024 * 1024)
    return pltpu.CompilerParams(dimension_semantics=sem, vmem_limit_bytes=limit)


def _choose_tiles(m, n, vmem_bytes):
    best = None
    for tm in ROW_TILES:
        if m % tm:
            continue
        for parts in range(1, n // V7X_MXU_COLS + 1):
            tn = n // parts
            if n % parts or tn % V7X_MXU_COLS:
                continue
            need = vmem_bytes(tm, tn)
            if need > VMEM_TILE_BUDGET:
                continue
            key = (tm * tn, tn == n, tm)
            if best is None or key > best[0]:
                best = (key, tm, tn, need)
    if best is None:
        raise ValueError("no matmul tile fits VMEM")
    return best[1:]


def _weight_spec(block, index_map, resident):
    if resident:
        return pl.BlockSpec(block, index_map, pipeline_mode=pl.Buffered(1))
    return pl.BlockSpec(block, index_map)


def _rmsnorm_rows(x, gain):
    inv = lax.rsqrt(jnp.mean(x * x, axis=-1, keepdims=True) + NORM_EPS)
    return x * inv * gain


def _norm_share(x_ref, gain_ref, h_ref, step, n_steps):
    tm = x_ref.shape[0]
    rows = -(-tm // n_steps)
    rows = min(tm, -(-rows // NORM_ALIGN) * NORM_ALIGN)
    start = pl.multiple_of(jnp.minimum(step * rows, tm - rows), NORM_ALIGN)
    for r in range(0, rows, NORM_ROWS):
        sl = pl.ds(start + r, min(NORM_ROWS, rows - r))
        h_ref[sl, :] = _rmsnorm_rows(x_ref[sl, :], gain_ref[...]).astype(BF16)


def _lagged_tiles(norm_into, matmul_from, bufs, n_tiles):
    i = pl.program_id(0)

    @pl.when(i == 0)
    def _():
        norm_into(bufs[0])

    for parity in (0, 1):
        @pl.when(jnp.logical_and(jnp.logical_and(i > 0, i < n_tiles), i % 2 == parity))
        def _(parity=parity):
            matmul_from(bufs[1 - parity])
            norm_into(bufs[parity])

    @pl.when(i == n_tiles)
    def _():
        matmul_from(bufs[(n_tiles - 1) % 2])


def _lagged_index_maps(n_tiles, layer):
    x_map = lambda i, j: (jnp.minimum(i, n_tiles - 1), 0)
    col = lambda i, j: jnp.where(i == 0, 0, j)
    w_map = lambda shift: (lambda i, j: (layer, 0, col(i, j) + shift))
    o_map = lambda i, j: (jnp.maximum(i - 1, 0), col(i, j))
    return x_map, w_map, o_map


def _norm_matmul_kernel(x_ref, gain_ref, w_ref, o_ref, ha_ref, hb_ref, *, n_tiles, n_cols):
    col = pl.program_id(1)

    def norm_into(h_ref):
        _norm_share(x_ref, gain_ref, h_ref, col, n_cols)

    def matmul_from(h_ref):
        o_ref[...] = jnp.dot(h_ref[...], w_ref[...], preferred_element_type=F32).astype(o_ref.dtype)

    _lagged_tiles(norm_into, matmul_from, (ha_ref, hb_ref), n_tiles)


def norm_matmul(x, gain, w, layer, *, tm=None, tn=None):
    m, d = x.shape
    n = w.shape[-1]

    def vmem_bytes(tm, tn):
        w_bufs = 1 if tn == n else 2
        return 2 * tm * d * 4 + 2 * tm * d * 2 + w_bufs * d * tn * 2 + 2 * tm * tn * 2

    if tm is None or tn is None:
        tm, tn, need = _choose_tiles(m, n, vmem_bytes)
    else:
        need = vmem_bytes(tm, tn)
    n_tiles, n_cols = m // tm, n // tn
    x_map, w_map, o_map = _lagged_index_maps(n_tiles, layer)
    return pl.pallas_call(
        functools.partial(_norm_matmul_kernel, n_tiles=n_tiles, n_cols=n_cols),
        grid=(n_tiles + 1, n_cols),
        in_specs=[
            pl.BlockSpec((tm, d), x_map),
            pl.BlockSpec((1, d), lambda i, j: (0, 0)),
            _weight_spec((None, d, tn), w_map(0), tn == n),
        ],
        out_specs=pl.BlockSpec((tm, tn), o_map),
        out_shape=jax.ShapeDtypeStruct((m, n), BF16),
        scratch_shapes=[pltpu.VMEM((tm, d), BF16), pltpu.VMEM((tm, d), BF16)],
        compiler_params=_params("arbitrary", "arbitrary", vmem_bytes=need),
        name="norm_matmul",
    )(x, gain, w)


def _norm_swiglu_kernel(x_ref, gain_ref, wg_ref, wu_ref, o_ref, ha_ref, hb_ref, *, n_tiles, n_cols):
    col = pl.program_id(1)

    def norm_into(h_ref):
        _norm_share(x_ref, gain_ref, h_ref, col, n_cols)

    def matmul_from(h_ref):
        h = h_ref[...]
        gate = jnp.dot(h, wg_ref[...], preferred_element_type=F32)
        up = jnp.dot(h, wu_ref[...], preferred_element_type=F32)
        o_ref[...] = (jax.nn.silu(gate) * up).astype(o_ref.dtype)

    _lagged_tiles(norm_into, matmul_from, (ha_ref, hb_ref), n_tiles)


def norm_swiglu(x, gain, w_in, layer, *, tm=None, tn=None):
    m, d = x.shape
    hd = w_in.shape[-1] // 2

    def vmem_bytes(tm, tn):
        return 2 * tm * d * 4 + 2 * tm * d * 2 + 2 * 2 * d * tn * 2 + 2 * tm * tn * 2

    if tm is None or tn is None:
        tm, tn, need = _choose_tiles(m, hd, vmem_bytes)
    else:
        need = vmem_bytes(tm, tn)
    n_tiles, n_cols = m // tm, hd // tn
    x_map, w_map, o_map = _lagged_index_maps(n_tiles, layer)
    return pl.pallas_call(
        functools.partial(_norm_swiglu_kernel, n_tiles=n_tiles, n_cols=n_cols),
        grid=(n_tiles + 1, n_cols),
        in_specs=[
            pl.BlockSpec((tm, d), x_map),
            pl.BlockSpec((1, d), lambda i, j: (0, 0)),
            pl.BlockSpec((None, d, tn), w_map(0)),
            pl.BlockSpec((None, d, tn), w_map(n_cols)),
        ],
        out_specs=pl.BlockSpec((tm, tn), o_map),
        out_shape=jax.ShapeDtypeStruct((m, hd), BF16),
        scratch_shapes=[pltpu.VMEM((tm, d), BF16), pltpu.VMEM((tm, d), BF16)],
        compiler_params=_params("arbitrary", "arbitrary", vmem_bytes=need),
        name="norm_swiglu",
    )(x, gain, w_in, w_in)


def _matmul_residual_kernel(a_ref, w_ref, x_ref, o_ref):
    o_ref[...] = x_ref[...] + jnp.dot(a_ref[...], w_ref[...], preferred_element_type=F32)


def matmul_residual(a, w, x, layer, *, tm=None, tn=None):
    m, k = a.shape
    n = w.shape[-1]

    def vmem_bytes(tm, tn):
        w_bufs = 1 if tn == n else 2
        return 2 * tm * k * 2 + w_bufs * k * tn * 2 + 2 * 2 * tm * tn * 4

    if tm is None or tn is None:
        tm, tn, need = _choose_tiles(m, n, vmem_bytes)
    else:
        need = vmem_bytes(tm, tn)
    return pl.pallas_call(
        _matmul_residual_kernel,
        grid=(m // tm, n // tn),
        in_specs=[
            pl.BlockSpec((tm, k), lambda i, j: (i, 0)),
            _weight_spec((None, k, tn), lambda i, j: (layer, 0, j), tn == n),
            pl.BlockSpec((tm, tn), lambda i, j: (i, j)),
        ],
        out_specs=pl.BlockSpec((tm, tn), lambda i, j: (i, j)),
        out_shape=jax.ShapeDtypeStruct((m, n), F32),
        compiler_params=_params("parallel", "arbitrary", vmem_bytes=need),
        name="matmul_residual",
    )(a, w, x)


def _ple_kernel(xn_ref, xr_ref, gain_ref, wg_ref, p_ref, wp_ref, fgain_ref, o_ref, ha_ref, hb_ref,
                *, n_tiles, final):
    tm, d = o_ref.shape

    def norm_into(h_ref):
        _norm_share(xn_ref, gain_ref, h_ref, 0, 1)

    def matmul_from(h_ref):
        h = h_ref[...]
        pb = p_ref[...].astype(BF16)
        for c in range(0, d, PLE_COLS):
            cols = slice(c, c + PLE_COLS)
            gate = jax.nn.sigmoid(jnp.dot(h, wg_ref[:, cols], preferred_element_type=F32))
            pp = jnp.dot(pb, wp_ref[:, cols], preferred_element_type=F32)
            o_ref[:, cols] = xr_ref[:, cols] + gate * pp
        if final:
            for r in range(0, tm, NORM_ROWS):
                rows = slice(r, r + NORM_ROWS)
                o_ref[rows, :] = _rmsnorm_rows(o_ref[rows, :], fgain_ref[...])

    _lagged_tiles(norm_into, matmul_from, (ha_ref, hb_ref), n_tiles)


def ple_update(x, gain, w_gate, p, w_proj, final_gain, layer, *, final, tm=None):
    m, d = x.shape
    pd = p.shape[-1]
    tm = tm or _pick(m, (512, 256, 128))
    n_tiles = m // tm
    need = 3 * 2 * tm * d * 4 + d * d * 2 + pd * d * 2 + 2 * tm * pd * 4 + 2 * tm * d * 2
    ahead = lambda i: (jnp.minimum(i, n_tiles - 1), 0)
    behind = lambda i: (jnp.maximum(i - 1, 0), 0)
    resident = lambda i: (layer, 0, 0)
    return pl.pallas_call(
        functools.partial(_ple_kernel, n_tiles=n_tiles, final=final),
        grid=(n_tiles + 1,),
        in_specs=[
            pl.BlockSpec((tm, d), ahead),
            pl.BlockSpec((tm, d), behind),
            pl.BlockSpec((1, d), lambda i: (0, 0)),
            _weight_spec((None, d, d), resident, True),
            pl.BlockSpec((None, tm, pd), lambda i: (layer, jnp.maximum(i - 1, 0), 0)),
            _weight_spec((None, pd, d), resident, True),
            pl.BlockSpec((1, d), lambda i: (0, 0)),
        ],
        out_specs=pl.BlockSpec((tm, d), behind),
        out_shape=jax.ShapeDtypeStruct((m, d), F32),
        scratch_shapes=[pltpu.VMEM((tm, d), BF16), pltpu.VMEM((tm, d), BF16)],
        compiler_params=_params("arbitrary", vmem_bytes=need),
        name="ple_update",
    )(x, x, gain, w_gate, p, w_proj, final_gain)


def _ret_core_kernel(q_ref, k_ref, v_ref, g_ref, dint_ref, qd_ref, kd_ref, cd_ref, gn_ref,
                     o_ref, state_ref, *, heads, chunk):
    dk = q_ref.shape[1] // heads
    dv = v_ref.shape[1] // heads

    @pl.when(pl.program_id(2) == 0)
    def _():
        state_ref[...] = jnp.zeros_like(state_ref)

    def body(ci, carry):
        sl = pl.ds(pl.multiple_of(ci * chunk, chunk), chunk)
        for hh in range(heads):
            ck = slice(hh * dk, (hh + 1) * dk)
            cv = slice(hh * dv, (hh + 1) * dv)
            q = q_ref[sl, ck]
            k = k_ref[sl, ck]
            v = v_ref[sl, cv]
            state = state_ref[hh]
            scores = lax.dot_general(q, k, (((1,), (1,)), ((), ())), preferred_element_type=F32)
            scores = (scores * dint_ref[hh]).astype(BF16)
            q_dec = q * qd_ref[hh]
            o = (jnp.dot(scores, v, preferred_element_type=F32)
                 + jnp.dot(q_dec, state.astype(BF16), preferred_element_type=F32))
            k_dec = k * kd_ref[hh]
            state_ref[hh] = state * cd_ref[hh] + lax.dot_general(
                k_dec, v, (((0,), (0,)), ((), ())), preferred_element_type=F32)
            mu = jnp.mean(o, axis=-1, keepdims=True)
            dev = o - mu
            var = jnp.mean(dev * dev, axis=-1, keepdims=True)
            yn = dev * lax.rsqrt(var + NORM_EPS) * gn_ref[hh]
            g = g_ref[sl, cv]
            o_ref[sl, cv] = (jax.nn.sigmoid(g) * (g * yn.astype(BF16))).astype(o_ref.dtype)
        return carry

    lax.fori_loop(0, q_ref.shape[0] // chunk, body, 0)


def _retention_decays(c, dk, dv):
    h = RET_HEADS
    log_g = jnp.log1p(-jnp.exp2(-5.0 - jnp.arange(h, dtype=F32)))
    pos = jnp.arange(c, dtype=F32)
    diff = pos[:, None] - pos[None, :]
    causal = diff >= 0
    decay_intra = jnp.where(causal, jnp.exp(log_g[:, None, None] * jnp.where(causal, diff, 0.0)), 0.0)
    q_decay = jnp.exp(log_g[:, None] * (pos + 1.0))[:, :, None]
    k_decay = jnp.exp(log_g[:, None] * (c - 1.0 - pos))[:, :, None]
    chunk_decay = jnp.exp(log_g * c)[:, None, None]
    k_scale = dk ** -0.5
    return (decay_intra * k_scale,
            jnp.broadcast_to(q_decay, (h, c, dk)).astype(BF16),
            jnp.broadcast_to(k_decay * k_scale, (h, c, dk)).astype(BF16),
            jnp.broadcast_to(chunk_decay, (h, 1, dv)))


def retention_core(proj, gn_gain, *, ts=None, heads=RET_HEADS_PER_STEP, chunk=RET_CHUNK):
    b, s, width = proj.shape
    h = RET_HEADS
    dk = width // (6 * h)
    dv = 2 * dk
    ts = ts or _pick(s, (512, 256, 128))
    chunk = min(chunk, ts)
    hg = h // heads
    dint, qd, kd, cd = _retention_decays(chunk, dk, dv)
    head_table = lambda last: pl.BlockSpec((heads,) + last, lambda bi, hi, ti: (hi, 0, 0))
    return pl.pallas_call(
        functools.partial(_ret_core_kernel, heads=heads, chunk=chunk),
        grid=(b, hg, s // ts),
        in_specs=[
            pl.BlockSpec((None, ts, heads * dk), lambda bi, hi, ti: (bi, ti, hi)),
            pl.BlockSpec((None, ts, heads * dk), lambda bi, hi, ti: (bi, ti, hg + hi)),
            pl.BlockSpec((None, ts, heads * dv), lambda bi, hi, ti: (bi, ti, hg + hi)),
            pl.BlockSpec((None, ts, heads * dv), lambda bi, hi, ti: (bi, ti, 2 * hg + hi)),
            head_table((chunk, chunk)),
            head_table((chunk, dk)),
            head_table((chunk, dk)),
            head_table((1, dv)),
            head_table((1, dv)),
        ],
        out_specs=pl.BlockSpec((None, ts, heads * dv), lambda bi, hi, ti: (bi, ti, hi)),
        out_shape=jax.ShapeDtypeStruct((b, s, h * dv), BF16),
        scratch_shapes=[pltpu.VMEM((heads, dk, dv), F32)],
        compiler_params=_params("parallel", "parallel", "arbitrary"),
        name="retention_core",
    )(proj, proj, proj, proj, dint, qd, kd, cd, gn_gain)


def _swa_core_kernel(sinks_ref, q_ref, kp_ref, kc_ref, vp_ref, vc_ref, bias_ref, o_ref,
                     kb_ref, vb_ref, *, n_heads):
    w, hd = SWA_WINDOW, SWA_HEAD_DIM
    group = n_heads // SWA_KV_HEADS
    ts = q_ref.shape[0]
    q_scale = hd ** -0.5
    kb_ref[0:w, :] = kp_ref[...] * q_scale
    kb_ref[w:, :] = kc_ref[...] * q_scale
    vb_ref[0:w, :] = vp_ref[...]
    vb_ref[w:, :] = vc_ref[...]

    row = lax.broadcasted_iota(jnp.int32, (w, w), 0)
    lane = lax.broadcasted_iota(jnp.int32, (w, w), 1)
    upper = lane > row
    left = lane < hd
    zero = jnp.zeros((w, w), BF16)
    upper2 = jnp.concatenate([upper, upper], axis=1)
    zero2 = jnp.zeros((w, 2 * w), BF16)
    ones_diag = jnp.concatenate([jnp.where(left, 1.0, 0.0), jnp.where(left, 0.0, 1.0)],
                                axis=0).astype(BF16)
    first_tile = pl.program_id(1) == 0
    nt = (((1,), (1,)), ((), ()))

    def split(slab):
        lhs = lax.broadcasted_iota(jnp.int32, slab.shape, 1) < hd
        nil = jnp.zeros(slab.shape, BF16)
        swapped = pltpu.roll(slab, hd, axis=1)
        return ((jnp.where(lhs, slab, nil), jnp.where(lhs, nil, swapped)),
                (jnp.where(lhs, swapped, nil), jnp.where(lhs, nil, slab)))

    def block_stages(blk):
        row0 = pl.multiple_of(blk * w, w)
        fidx = jnp.logical_and(first_tile, blk == 0).astype(jnp.int32)
        rows = pl.ds(row0, w)
        prev_rows = pl.ds(row0, w)
        cur_rows = pl.ds(row0 + w, w)
        kband, vprev, vcur = [], [], []
        for slab_i in range(SWA_KV_HEADS // 2):
            cols = slice(slab_i * 2 * hd, (slab_i + 1) * 2 * hd)
            kband.extend(split(kb_ref[pl.ds(row0, 2 * w), cols]))
            vprev.extend(jnp.concatenate([jnp.concatenate(t, axis=0), ones_diag], axis=1)
                         for t in split(vb_ref[prev_rows, cols]))
            vcur.extend(jnp.concatenate([jnp.concatenate(t, axis=0), ones_diag], axis=1)
                        for t in split(vb_ref[cur_rows, cols]))

        def scores(pair):
            kv = (2 * pair) // group
            qp = q_ref[rows, 2 * pair * hd:(2 * pair + 2) * hd]
            out = []
            for side in range(2):
                s2 = lax.dot_general(qp, kband[kv][side], nt, preferred_element_type=F32)
                out.append(jnp.where(upper, s2[:, :w], s2[:, w:]) + bias_ref[fidx, 2 * pair + side])
            return out

        def softmax_numerators(pair, s_pair):
            out = []
            for side in range(2):
                sink = sinks_ref[2 * pair + side]
                m = jnp.maximum(jnp.max(s_pair[side], axis=-1, keepdims=True), sink)
                out.append((jnp.exp(s_pair[side] - m).astype(BF16), jnp.exp(sink - m)))
            return out

        def weighted_values(pair, e_pair):
            kv = (2 * pair) // group
            e = jnp.concatenate([e_pair[0][0], e_pair[1][0]], axis=1)
            both = (jnp.dot(jnp.where(upper2, e, zero2), vprev[kv], preferred_element_type=F32)
                    + jnp.dot(jnp.where(upper2, zero2, e), vcur[kv], preferred_element_type=F32))
            acc = both[:, :w]
            den = both[:, w:] + jnp.where(left, e_pair[0][1], e_pair[1][1])
            o_ref[rows, 2 * pair * hd:(2 * pair + 2) * hd] = (acc * (1.0 / den)).astype(o_ref.dtype)

        return scores, softmax_numerators, weighted_values

    def body(n, carry):
        stages = [block_stages(n * SWA_BLOCKS_PER_ITER + u) for u in range(SWA_BLOCKS_PER_ITER)]
        n_pairs = n_heads // 2
        items = [(u, pair) for u in range(SWA_BLOCKS_PER_ITER) for pair in range(n_pairs)]
        s_tiles, e_tiles = {}, {}
        for t in range(len(items) + 2 * SWA_LOOKAHEAD):
            if t < len(items):
                u, pair = items[t]
                s_tiles[t] = stages[u][0](pair)
            t1 = t - SWA_LOOKAHEAD
            if 0 <= t1 < len(items):
                u, pair = items[t1]
                e_tiles[t1] = stages[u][1](pair, s_tiles.pop(t1))
            t2 = t1 - SWA_LOOKAHEAD
            if 0 <= t2 < len(items):
                u, pair = items[t2]
                stages[u][2](pair, e_tiles.pop(t2))
        return carry

    lax.fori_loop(0, ts // (w * SWA_BLOCKS_PER_ITER), body, 0)


def _swa_bias(n_heads):
    w = SWA_WINDOW
    slopes = jnp.exp2(-8.0 * (jnp.arange(n_heads, dtype=F32) + 1.0) / n_heads)
    qi = jnp.arange(w)[:, None]
    kc = jnp.arange(w)[None, :]
    upper = kc > qi
    rel = jnp.where(upper, qi - kc + w, qi - kc).astype(F32)
    bias = -slopes[:, None, None] * rel
    return jnp.stack([bias, jnp.where(upper, -jnp.inf, bias)])


def swa_core(proj, sinks, *, ts=None):
    b, s, width = proj.shape
    w = SWA_WINDOW
    kvw = SWA_KV_HEADS * SWA_HEAD_DIM
    qw = width - 2 * kvw
    n_heads = qw // SWA_HEAD_DIM
    ts = ts or _pick(s, (512, 256, 128))
    bpt = ts // w
    prev = lambda col: (lambda bi, ti: (bi, jnp.maximum(ti * bpt - 1, 0), col))
    cur = lambda col: (lambda bi, ti: (bi, ti, col))
    return pl.pallas_call(
        functools.partial(_swa_core_kernel, n_heads=n_heads),
        grid=(b, s // ts),
        in_specs=[
            pl.BlockSpec(memory_space=pltpu.SMEM),
            pl.BlockSpec((None, ts, qw), lambda bi, ti: (bi, ti, 0)),
            pl.BlockSpec((None, w, kvw), prev(qw // kvw)),
            pl.BlockSpec((None, ts, kvw), cur(qw // kvw)),
            pl.BlockSpec((None, w, kvw), prev(qw // kvw + 1)),
            pl.BlockSpec((None, ts, kvw), cur(qw // kvw + 1)),
            pl.BlockSpec((2, n_heads, w, w), lambda bi, ti: (0, 0, 0, 0)),
        ],
        out_specs=pl.BlockSpec((None, ts, qw), lambda bi, ti: (bi, ti, 0)),
        out_shape=jax.ShapeDtypeStruct((b, s, qw), BF16),
        scratch_shapes=[pltpu.VMEM((ts + w, kvw), BF16), pltpu.VMEM((ts + w, kvw), BF16)],
        compiler_params=_params("parallel", "arbitrary"),
        name="swa_core",
    )(sinks, proj, proj, proj, proj, proj, _swa_bias(n_heads))


def kernel(x, p, norm_mix, norm_ffn, norm_ple, norm_final, ret_w_in, ret_gn, ret_w_out,
           swa_w_in, swa_sinks, swa_w_out, ffn_w_in, ffn_w_out, ple_w_proj, ple_w_gate):
    b, s, d = x.shape
    depth = p.shape[0]
    m = b * s
    ret_w_in, ret_w_out, swa_w_in, swa_w_out, ffn_w_in, ffn_w_out, ple_w_proj, ple_w_gate = (
        a.astype(BF16) for a in (ret_w_in, ret_w_out, swa_w_in, swa_w_out, ffn_w_in, ffn_w_out,
                                 ple_w_proj, ple_w_gate))
    p = p.reshape(depth, m, p.shape[-1])
    x = x.reshape(m, d)
    final_gain = norm_final[None, :]
    for i in range(depth):
        j = i // 2
        if i % 2 == 0:
            proj = norm_matmul(x, norm_mix[i][None, :], ret_w_in, j)
            y = retention_core(proj.reshape(b, s, -1), ret_gn[j][:, None, :])
            x = matmul_residual(y.reshape(m, -1), ret_w_out, x, j)
        else:
            proj = norm_matmul(x, norm_mix[i][None, :], swa_w_in, j)
            y = swa_core(proj.reshape(b, s, -1), swa_sinks[j])
            x = matmul_residual(y.reshape(m, -1), swa_w_out, x, j)
        hidden = norm_swiglu(x, norm_ffn[i][None, :], ffn_w_in, i)
        x = matmul_residual(hidden, ffn_w_out, x, i)
        x = ple_update(x, norm_ple[i][None, :], ple_w_gate, p, ple_w_proj, final_gain, i,
                       final=(i == depth - 1))
    return x.reshape(b, s, d)
```
